```python
import math
import jax, jax.numpy as jnp
from jax import lax
import numpy as np

D_MODEL = 1024
BATCH = 8
SEQ = 4096
DEPTH = 2
DEC_BATCH = 32
DEC_SEQ = 8
PAST_LEN = 16384
PAGE_SIZE = 128

N_EVEN = (DEPTH + 1) // 2
N_ODD = DEPTH // 2
D_A = D_MODEL // 2
A_GROUPS = 8
CONV_W = 3
H_B = 4
DK = 64
DV = 2 * DK
D_B = H_B * DV
QK_W = H_B * 2 * DK
IN_EVEN = 3 * D_A + 2 * QK_W + D_B
CHUNK = 128
D_C = D_MODEL
G_C = 8
D_FF = (((8 * D_MODEL + 2) // 3 + 255) // 256) * 256
Q_BLOCK = 128
EPS = 1e-6

kernel_name = "hybrid_conv_diffattn_chunkmlp_decoder_step"


def rms_norm(x, g):
    xf = x.astype(jnp.float32)
    y = xf * lax.rsqrt(jnp.mean(xf * xf, axis=-1, keepdims=True) + EPS)
    return (y * g.astype(jnp.float32)).astype(x.dtype)


def layer_norm(x, g, b):
    xf = x.astype(jnp.float32)
    mu = jnp.mean(xf, axis=-1, keepdims=True)
    var = jnp.mean(jnp.square(xf - mu), axis=-1, keepdims=True)
    y = (xf - mu) * lax.rsqrt(var + EPS)
    return (y * g.astype(jnp.float32) + b.astype(jnp.float32)).astype(x.dtype)


def adaln(c, w, b):
    mod = (jax.nn.silu(c) @ w + b)[:, None, :]
    return jnp.split(mod, 6, axis=-1)


def modulate(h, shift, scale):
    return h * (1.0 + scale) + shift


def short_conv(xc, state, w):
    T = xc.shape[1]
    xp = jnp.concatenate([state, xc], axis=1)
    y = w[0] * xp[:, 0:T]
    for j in range(1, CONV_W):
        y = y + w[j] * xp[:, j:j + T]
    return y, xp[:, -(CONV_W - 1):]


def even_project(h, w_in, conv_w, conv_state, q_g, k_g):
    B, T, _ = h.shape
    z = h @ w_in
    a_b, a_c, a_x, q, k, v = jnp.split(
        z, [D_A, 2 * D_A, 3 * D_A, 3 * D_A + QK_W, 3 * D_A + 2 * QK_W], axis=-1)
    y_conv, new_state = short_conv(a_c * a_x, conv_state, conv_w)
    a_out = a_b * y_conv
    q = rms_norm(q.reshape(B, T, H_B, 2, DK), q_g)
    k = rms_norm(k.reshape(B, T, H_B, 2, DK), k_g)
    v = v.reshape(B, T, H_B, DV)
    return a_out, q, k, v, new_state


def diff_lambda(lq1, lk1, lq2, lk2, lam_init):
    f = jnp.float32
    return (jnp.exp(jnp.sum(lq1.astype(f) * lk1.astype(f)))
            - jnp.exp(jnp.sum(lq2.astype(f) * lk2.astype(f))) + lam_init)


def diff_attention(q, k, v, lam, q_pos, k_pos):
    s = jnp.einsum('bqhmd,bkhmd->bhmqk', q, k,
                   preferred_element_type=jnp.float32) * (DK ** -0.5)
    mask = k_pos[None, :] <= q_pos[:, None]
    s = jnp.where(mask, s, -jnp.inf)
    p = jax.nn.softmax(s, axis=-1)
    p_diff = p[:, :, 0] - lam * p[:, :, 1]
    return jnp.einsum('bhqk,bkhd->bqhd', p_diff.astype(v.dtype), v)


def prompt_diff_attention(q, k, v, lam):
    B, T = q.shape[:2]
    nb = T // Q_BLOCK
    qb = q.reshape(B, nb, Q_BLOCK, H_B, 2, DK).swapaxes(0, 1)
    k_pos = jnp.arange(T)

    def one_block(args):
        q_blk, bi = args
        q_pos = bi * Q_BLOCK + jnp.arange(Q_BLOCK)
        return diff_attention(q_blk, k, v, lam, q_pos, k_pos)

    o = lax.map(one_block, (qb, jnp.arange(nb)))
    return o.swapaxes(0, 1).reshape(B, T, H_B, DV)


def sample_diff_attention(q, k_new, v_new, k_past, v_past, lam):
    past_len, T = k_past.shape[1], q.shape[1]
    k_all = jnp.concatenate([k_past, k_new], axis=1)
    v_all = jnp.concatenate([v_past, v_new], axis=1)
    q_pos = past_len + jnp.arange(T)
    k_pos = jnp.arange(past_len + T)
    return diff_attention(q, k_all, v_all, lam, q_pos, k_pos)


def even_merge(a_out, o, subln_g, lam_init, w_out):
    B, T = a_out.shape[:2]
    o = rms_norm(o, subln_g) * (1.0 - lam_init)
    return jnp.concatenate([a_out, o.reshape(B, T, D_B)], axis=-1) @ w_out


def chunk_mlp(h, w_in, b_in, ln_g, ln_b, w_s, b_s, w_out):
    B, T, _ = h.shape
    z = jax.nn.gelu(h @ w_in + b_in)
    u, v = jnp.split(z, 2, axis=-1)
    v = layer_norm(v, ln_g, ln_b)
    T_pad = -(-T // CHUNK) * CHUNK
    vp = jnp.pad(v, ((0, 0), (0, T_pad - T), (0, 0)))
    vp = vp.reshape(B, T_pad // CHUNK, CHUNK, G_C, D_C // G_C)
    ws = w_s * jnp.tril(jnp.ones((CHUNK, CHUNK), w_s.dtype))
    mixed = jnp.einsum('gij,bnjgc->bnigc', ws, vp) + b_s.T[None, None, :, :, None]
    mixed = mixed.reshape(B, T_pad, D_C)[:, :T]
    return (u * mixed) @ w_out, v


def swiglu(h, w_gu, w_down):
    g, u = jnp.split(h @ w_gu, 2, axis=-1)
    return (jax.nn.silu(g) * u) @ w_down


def setup_inputs(seed: int = 0) -> dict:
    key = jax.random.key(seed)
    ks = iter(jax.random.split(key, 40))
    f = jnp.float32

    def nrm(shape, scale):
        return jax.random.normal(next(ks), shape, f) * scale

    n_pages = PAST_LEN // PAGE_SIZE
    n_used = DEC_BATCH * n_pages
    n_pool = (n_used * 5 + 3) // 4
    page_table = jax.random.permutation(next(ks), n_pool)[:n_used].reshape(
        DEC_BATCH, n_pages).astype(jnp.int32)
    return {
        "x_prompt": nrm((BATCH, SEQ, D_MODEL), 1.0),
        "x_sample": nrm((DEC_BATCH, DEC_SEQ, D_MODEL), 1.0),
        "c_prompt": nrm((BATCH, D_MODEL), 1.0),
        "c_sample": nrm((DEC_BATCH, D_MODEL), 1.0),
        "cache_k": nrm((N_EVEN, n_pool, PAGE_SIZE, H_B, 2 * DK), 1.0),
        "cache_v": nrm((N_EVEN, n_pool, PAGE_SIZE, H_B, DV), 1.0),
        "state_conv": nrm((N_EVEN, DEC_BATCH, CONV_W - 1, D_A), 1.0),
        "page_table": page_table,
        "w_ada": nrm((DEPTH, D_MODEL, 6 * D_MODEL), 0.5 * D_MODEL ** -0.5),
        "b_ada": nrm((DEPTH, 6 * D_MODEL), 0.01),
        "norm_g": 1.0 + nrm((DEPTH, 2, D_MODEL), 0.01),
        "w_in_even": nrm((N_EVEN, D_MODEL, IN_EVEN), D_MODEL ** -0.5),
        "conv_w": nrm((N_EVEN, CONV_W, D_A), CONV_W ** -0.5),
        "q_norm_g": 1.0 + nrm((N_EVEN, DK), 0.01),
        "k_norm_g": 1.0 + nrm((N_EVEN, DK), 0.01),
        "lam_q1": nrm((N_EVEN, DK), 0.1),
        "lam_k1": nrm((N_EVEN, DK), 0.1),
        "lam_q2": nrm((N_EVEN, DK), 0.1),
        "lam_k2": nrm((N_EVEN, DK), 0.1),
        "subln_g": 1.0 + nrm((N_EVEN, DV), 0.01),
        "w_out_even": nrm((N_EVEN, D_A + D_B, D_MODEL), (D_A + D_B) ** -0.5),
        "w_in_odd": nrm((N_ODD, D_MODEL, 2 * D_C), D_MODEL ** -0.5),
        "b_in_odd": nrm((N_ODD, 2 * D_C), 0.01),
        "ln_g": 1.0 + nrm((N_ODD, D_C), 0.01),
        "ln_b": nrm((N_ODD, D_C), 0.01),
        "w_s": nrm((N_ODD, G_C, CHUNK, CHUNK), CHUNK ** -0.5),
        "b_s": 1.0 + nrm((N_ODD, G_C, CHUNK), 0.01),
        "w_out_odd": nrm((N_ODD, D_C, D_MODEL), D_C ** -0.5),
        "w_gate_up": nrm((DEPTH, D_MODEL, 2 * D_FF), D_MODEL ** -0.5),
        "w_down": nrm((DEPTH, D_FF, D_MODEL), D_FF ** -0.5),
    }


def reference(x_prompt, x_sample, c_prompt, c_sample, cache_k, cache_v, state_conv,
              page_table, w_ada, b_ada, norm_g, w_in_even, conv_w, q_norm_g, k_norm_g,
              lam_q1, lam_k1, lam_q2, lam_k2, subln_g, w_out_even, w_in_odd, b_in_odd,
              ln_g, ln_b, w_s, b_s, w_out_odd, w_gate_up, w_down):
    xp, xs = x_prompt, x_sample
    Bp, Tp = xp.shape[:2]
    Bs, Ts = xs.shape[:2]
    past_len = page_table.shape[1] * cache_k.shape[2]
    k_p_list, v_p_list, conv_p_list = [], [], []
    k_s_list, v_s_list, conv_s_list, chunk_v_list = [], [], [], []

    for i in range(DEPTH):
        sh_p, sc_p, g_p, shf_p, scf_p, gf_p = adaln(c_prompt, w_ada[i], b_ada[i])
        sh_s, sc_s, g_s, shf_s, scf_s, gf_s = adaln(c_sample, w_ada[i], b_ada[i])
        hp = modulate(rms_norm(xp, norm_g[i, 0]), sh_p, sc_p)
        hs = modulate(rms_norm(xs, norm_g[i, 0]), sh_s, sc_s)
        if i % 2 == 0:
            e = i // 2
            lam_init = 0.8 - 0.6 * math.exp(-0.3 * i)
            lam = diff_lambda(lam_q1[e], lam_k1[e], lam_q2[e], lam_k2[e], lam_init)
            a_p, q_p, k_p, v_p, cst_p = even_project(
                hp, w_in_even[e], conv_w[e], jnp.zeros((Bp, CONV_W - 1, D_A), hp.dtype),
                q_norm_g[e], k_norm_g[e])
            o_p = prompt_diff_attention(q_p, k_p, v_p, lam)
            m_p = even_merge(a_p, o_p, subln_g[e], lam_init, w_out_even[e])
            a_s, q_s, k_s, v_s, cst_s = even_project(
                hs, w_in_even[e], conv_w[e], state_conv[e], q_norm_g[e], k_norm_g[e])
            k_past = cache_k[e, page_table].reshape(Bs, past_len, H_B, 2, DK)
            v_past = cache_v[e, page_table].reshape(Bs, past_len, H_B, DV)
            o_s = sample_diff_attention(q_s, k_s, v_s, k_past, v_past, lam)
            m_s = even_merge(a_s, o_s, subln_g[e], lam_init, w_out_even[e])
            k_p_list.append(k_p.reshape(Bp, Tp, H_B, 2 * DK))
            v_p_list.append(v_p)
            conv_p_list.append(cst_p)
            k_s_list.append(k_s.reshape(Bs, Ts, H_B, 2 * DK))
            v_s_list.append(v_s)
            conv_s_list.append(cst_s)
        else:
            o = i // 2
            m_p, _ = chunk_mlp(hp, w_in_odd[o], b_in_odd[o], ln_g[o], ln_b[o],
                               w_s[o], b_s[o], w_out_odd[o])
            m_s, v_rows = chunk_mlp(hs, w_in_odd[o], b_in_odd[o], ln_g[o], ln_b[o],
                                    w_s[o], b_s[o], w_out_odd[o])
            chunk_v_list.append(v_rows)
        xp = xp + g_p * m_p
        xs = xs + g_s * m_s
        hp = modulate(rms_norm(xp, norm_g[i, 1]), shf_p, scf_p)
        hs = modulate(rms_norm(xs, norm_g[i, 1]), shf_s, scf_s)
        xp = xp + gf_p * swiglu(hp, w_gate_up[i], w_down[i])
        xs = xs + gf_s * swiglu(hs, w_gate_up[i], w_down[i])

    k_prompt = jnp.stack(k_p_list)
    v_prompt = jnp.stack(v_p_list)
    conv_prompt = jnp.stack(conv_p_list)
    k_sample = jnp.stack(k_s_list)
    v_sample = jnp.stack(v_s_list)
    conv_sample = jnp.stack(conv_s_list)
    chunk_v_sample = jnp.stack(chunk_v_list)
    return (xp, xs, k_prompt, v_prompt, conv_prompt, k_sample, v_sample, conv_sample, chunk_v_sample)
```

```python
import functools
import math

import jax
import jax.numpy as jnp
from jax import lax
from jax.experimental import pallas as pl
from jax.experimental.pallas import tpu as pltpu

F32 = jnp.float32
BF16 = jnp.bfloat16
EPS = 1e-6

A_GROUPS = 8
CONV_W = 3
H_B = 4
DK = 64
DV = 2 * DK
CHUNK = 128
G_C = 8

VMEM_LIMIT_BYTES = 56 * 1024 * 1024

ROW_TILE = 512
ATTN_TILE = 512
PAGES_PER_STEP = 8


def _cparams(n_axes):
    return pltpu.CompilerParams(
        dimension_semantics=("arbitrary",) * n_axes,
        vmem_limit_bytes=VMEM_LIMIT_BYTES)


def _const_spec(shape):
    zeros = (0,) * len(shape)
    return pl.BlockSpec(shape, lambda *_: zeros, pipeline_mode=pl.Buffered(1))


def _rows_spec(tm, width, per_row=True):
    if per_row:
        return pl.BlockSpec((None, tm, width), lambda b, i: (b, i, 0))
    return pl.BlockSpec((None, 1, width), lambda b, i: (b, 0, 0))


def _dot(a, b):
    return jnp.dot(a, b, preferred_element_type=F32)


def _dot_nt(a, b):
    return lax.dot_general(a, b, (((1,), (1,)), ((), ())), preferred_element_type=F32)


def _rms_mod(x, g, shift, scale):
    ms = jnp.mean(x * x, axis=-1, keepdims=True)
    y = x * lax.rsqrt(ms + EPS) * g
    return y * (1.0 + scale) + shift


def _shr(x, pow2):
    shift = pow2.bit_length() - 1
    assert 1 << shift == pow2
    return jnp.right_shift(x, shift)


def _silu(x):
    return x / (1.0 + jnp.exp(-x))


def _gelu_tanh(x):
    c = math.sqrt(2.0 / math.pi)
    return x * (0.5 * (1.0 + jnp.tanh(c * (x + 0.044715 * (x * x * x)))))


def _ffn(x1, ng, shf, scf, gf, wgu_ref, wd_ref):
    d_ff = wd_ref.shape[0]
    n_chunks = 2 if (d_ff // 2) % 128 == 0 else 1
    ch = d_ff // n_chunks
    h = _rms_mod(x1, ng, shf, scf).astype(BF16)
    acc = None
    for c in range(n_chunks):
        g = _dot(h, wgu_ref[:, c * ch:(c + 1) * ch])
        u = _dot(h, wgu_ref[:, d_ff + c * ch:d_ff + (c + 1) * ch])
        a = (_silu(g) * u).astype(BF16)
        d = _dot(a, wd_ref[c * ch:(c + 1) * ch, :])
        acc = d if acc is None else acc + d
    return x1 + gf * acc


def _adaln_kernel(c_ref, w_ref, b_ref, o_ref):
    s = _silu(c_ref[...])
    o_ref[...] = jnp.dot(s, w_ref[...], precision=lax.Precision.HIGHEST,
                         preferred_element_type=F32) + b_ref[...]


def _adaln(c_all, w_ada, b_ada):
    depth, d, d6 = w_ada.shape
    r = c_all.shape[0]
    tn = d6 // 4
    return pl.pallas_call(
        _adaln_kernel,
        grid=(depth, d6 // tn),
        in_specs=[
            pl.BlockSpec((r, d), lambda l, j: (0, 0)),
            pl.BlockSpec((None, d, tn), lambda l, j: (l, 0, j)),
            pl.BlockSpec((None, 1, tn), lambda l, j: (l, 0, j)),
        ],
        out_specs=pl.BlockSpec((None, r, tn), lambda l, j: (l, 0, j)),
        out_shape=jax.ShapeDtypeStruct((depth, r, d6), F32),
        compiler_params=_cparams(2),
        name="adaln",
    )(c_all, w_ada, b_ada.reshape(depth, 1, d6))


def _inproj_kernel(*refs, tm, d_a, qk_w, has_fill, period):
    if has_fill:
        (x_ref, sh_ref, sc_ref, ng_ref, w_ref, cw_ref, qg_ref, kg_ref, bd_ref, f1_ref, f2_ref,
         a_ref, q_ref, k_ref, v_ref, kb_ref, vb_ref, tail_ref) = refs
    else:
        (x_ref, sh_ref, sc_ref, ng_ref, w_ref, cw_ref, qg_ref, kg_ref, bd_ref,
         a_ref, q_ref, k_ref, v_ref, kb_ref, vb_ref, tail_ref, carry_ref) = refs

    h = _rms_mod(x_ref[...], ng_ref[...], sh_ref[...], sc_ref[...]).astype(BF16)

    a_b = _dot(h, w_ref[:, 0:d_a])
    xc = _dot(h, w_ref[:, d_a:2 * d_a]) * _dot(h, w_ref[:, 2 * d_a:3 * d_a])
    row = lax.broadcasted_iota(jnp.int32, (tm, d_a), 0)
    if has_fill:
        rp = jnp.bitwise_and(row, period - 1)
        fill1 = f1_ref[...]
        fill2 = f2_ref[...]
    else:
        rp = row

        @pl.when(pl.program_id(1) == 0)
        def _():
            carry_ref[...] = jnp.zeros_like(carry_ref)

        last1 = carry_ref[pl.ds(7, 1), :]
        last2 = carry_ref[pl.ds(6, 1), :]
        fill1 = jnp.broadcast_to(last1, (tm, d_a))
        fill2 = jnp.where(row == 0, last2, last1)
    prev1 = jnp.where(rp < 1, fill1, pltpu.roll(xc, 1, 0))
    prev2 = jnp.where(rp < 2, fill2, pltpu.roll(xc, 2, 0))
    y = cw_ref[0:1, :] * prev2 + cw_ref[1:2, :] * prev1 + cw_ref[2:3, :] * xc
    a_ref[...] = (a_b * y).astype(BF16)
    tail_rows = tail_ref.shape[0]
    tail_ref[...] = xc[tm - tail_rows:tm, :]
    if not has_fill:
        carry_ref[...] = xc[tm - 8:tm, :]

    q0 = 3 * d_a
    zq = _dot(h, w_ref[:, q0:q0 + qk_w])
    msq = _dot((zq * zq).astype(BF16), bd_ref[...])
    qn = zq * lax.rsqrt(msq + EPS) * qg_ref[...]
    q_ref[...] = (qn * (DK ** -0.5)).astype(BF16)
    zk = _dot(h, w_ref[:, q0 + qk_w:q0 + 2 * qk_w])
    msk = _dot((zk * zk).astype(BF16), bd_ref[...])
    kn = zk * lax.rsqrt(msk + EPS) * kg_ref[...]
    k_ref[...] = kn
    kb_ref[...] = kn.astype(BF16)
    zv = _dot(h, w_ref[:, q0 + 2 * qk_w:])
    v_ref[...] = zv
    vb_ref[...] = zv.astype(BF16)


def _inproj(x, sh, sc, ng, w_in, cw, qg, kg, bd, fills, tm):
    nb, t, d = x.shape
    d_a = cw.shape[1]
    qk_w = qg.shape[1]
    d_b = w_in.shape[1] - 3 * d_a - 2 * qk_w
    has_fill = fills is not None
    per_row = sh.shape[1] != 1
    tail_rows = tm if has_fill else 8
    in_specs = [
        _rows_spec(tm, d), _rows_spec(tm, d, per_row), _rows_spec(tm, d, per_row),
        _const_spec(ng.shape), _const_spec(w_in.shape), _const_spec(cw.shape),
        _const_spec(qg.shape), _const_spec(kg.shape), _const_spec(bd.shape),
    ]
    args = [x, sh, sc, ng, w_in, cw, qg, kg, bd]
    scratch = []
    if has_fill:
        in_specs += [_rows_spec(tm, d_a), _rows_spec(tm, d_a)]
        args += list(fills)
        tail_spec = _rows_spec(tm, d_a)
        tail_shape = (nb, t, d_a)
    else:
        scratch = [pltpu.VMEM((8, d_a), F32)]
        tail_spec = pl.BlockSpec((None, 8, d_a), lambda b, i: (b, 0, 0))
        tail_shape = (nb, 8, d_a)
    out_specs = [_rows_spec(tm, d_a), _rows_spec(tm, qk_w), _rows_spec(tm, qk_w), _rows_spec(tm, d_b),
                 _rows_spec(tm, qk_w), _rows_spec(tm, d_b), tail_spec]
    out_shape = [
        jax.ShapeDtypeStruct((nb, t, d_a), BF16), jax.ShapeDtypeStruct((nb, t, qk_w), BF16),
        jax.ShapeDtypeStruct((nb, t, qk_w), F32), jax.ShapeDtypeStruct((nb, t, d_b), F32),
        jax.ShapeDtypeStruct((nb, t, qk_w), BF16), jax.ShapeDtypeStruct((nb, t, d_b), BF16),
        jax.ShapeDtypeStruct(tail_shape, F32),
    ]
    return pl.pallas_call(
        functools.partial(_inproj_kernel, tm=tm, d_a=d_a, qk_w=qk_w, has_fill=has_fill, period=8),
        grid=(nb, t // tm),
        in_specs=in_specs, out_specs=out_specs, out_shape=out_shape,
        scratch_shapes=scratch,
        compiler_params=_cparams(2),
        name="inproj",
    )(*args)


def _diff_lambda(lam_ref, lam_init):
    lv = lam_ref[...]
    e1 = jnp.exp(jnp.sum(lv[0:1, :] * lv[1:2, :], axis=-1, keepdims=True))
    e2 = jnp.exp(jnp.sum(lv[2:3, :] * lv[3:4, :], axis=-1, keepdims=True))
    return e1 - e2 + lam_init


def _subln(o, g, lam_init):
    ms = jnp.mean(o * o, axis=-1, keepdims=True)
    return o * lax.rsqrt(ms + EPS) * g * (1.0 - lam_init)


def _softmax_step(s, v_dot, m_ref, l_ref, acc_ref):
    m_prev = m_ref[...]
    m_new = jnp.maximum(m_prev, jnp.max(s, axis=-1, keepdims=True))
    alpha = jnp.exp(m_prev - m_new)
    p = jnp.exp(s - m_new)
    l_ref[...] = alpha * l_ref[...] + jnp.sum(p, axis=-1, keepdims=True)
    acc_ref[...] = alpha * acc_ref[...] + v_dot(p.astype(BF16))
    m_ref[...] = m_new


def _attn_kernel(q_ref, k_ref, v_ref, lam_ref, sg_ref, o_ref, qs_ref, m_ref, l_ref, acc_ref,
                 *, tq, lam_init):
    i = pl.program_id(2)
    q = q_ref[...]
    lane = lax.broadcasted_iota(jnp.int32, q.shape, 1)
    zero = jnp.zeros_like(q)
    qs_ref[0:tq, :] = jnp.where(lane < DK, q, zero)
    qs_ref[tq:2 * tq, :] = jnp.where(lane >= DK, q, zero)
    m_ref[...] = jnp.full_like(m_ref, -jnp.inf)
    l_ref[...] = jnp.zeros_like(l_ref)
    acc_ref[...] = jnp.zeros_like(acc_ref)

    def step(j, diagonal):
        start = pl.multiple_of(j * tq, tq)
        k = k_ref[pl.ds(start, tq), :]
        v = v_ref[pl.ds(start, tq), :]
        s = _dot_nt(qs_ref[...], k)
        if diagonal:
            r = lax.broadcasted_iota(jnp.int32, s.shape, 0)
            c = lax.broadcasted_iota(jnp.int32, s.shape, 1)
            s = jnp.where(c <= jnp.where(r >= tq, r - tq, r), s, -jnp.inf)
        _softmax_step(s, lambda p: _dot(p, v), m_ref, l_ref, acc_ref)

    def body(j, carry):
        step(j, False)
        return carry

    lax.fori_loop(0, i, body, 0)
    step(i, True)

    lam = _diff_lambda(lam_ref, lam_init)
    n = acc_ref[...] / l_ref[...]
    o = n[0:tq, :] - lam * n[tq:2 * tq, :]
    o_ref[...] = _subln(o, sg_ref[...], lam_init).astype(o_ref.dtype)


def _attn(q, kb, vb, lamv, sg, lam_init, tq):
    b, t, w = q.shape
    heads = w // DV
    return pl.pallas_call(
        functools.partial(_attn_kernel, tq=tq, lam_init=lam_init),
        grid=(b, heads, t // tq),
        in_specs=[
            pl.BlockSpec((None, tq, DV), lambda b_, h, i: (b_, i, h)),
            pl.BlockSpec((None, t, DV), lambda b_, h, i: (b_, 0, h)),
            pl.BlockSpec((None, t, DV), lambda b_, h, i: (b_, 0, h)),
            _const_spec(lamv.shape), _const_spec(sg.shape),
        ],
        out_specs=pl.BlockSpec((None, tq, DV), lambda b_, h, i: (b_, i, h)),
        out_shape=jax.ShapeDtypeStruct((b, t, w), BF16),
        scratch_shapes=[
            pltpu.VMEM((2 * tq, DV), BF16),
            pltpu.VMEM((2 * tq, 1), F32), pltpu.VMEM((2 * tq, 1), F32),
            pltpu.VMEM((2 * tq, DV), F32),
        ],
        compiler_params=_cparams(3),
        name="attn",
    )(q, kb, vb, lamv, sg)


def _sattn_kernel(pt_ref, q_ref, kn_ref, vn_ref, lam_ref, sg_ref, *rest, pp, ts, lam_init):
    k_refs = rest[0:pp]
    v_refs = rest[pp:2 * pp]
    o_ref, qbd_ref, m_ref, l_ref, acc_ref = rest[2 * pp:]
    j = pl.program_id(1)
    rows = qbd_ref.shape[0]
    width = qbd_ref.shape[1]
    page = k_refs[0].shape[0]

    @pl.when(j == 0)
    def _():
        q = q_ref[...]
        qt = jnp.concatenate([q] * (rows // ts), axis=0)
        r = lax.broadcasted_iota(jnp.int32, (rows, width), 0)
        c = lax.broadcasted_iota(jnp.int32, (rows, width), 1)
        same = _shr(r, ts) == _shr(c, DK)
        qbd_ref[...] = jnp.where(same, qt, 0.0).astype(BF16)
        m_ref[...] = jnp.full_like(m_ref, -jnp.inf)
        l_ref[...] = jnp.zeros_like(l_ref)
        acc_ref[...] = jnp.zeros_like(acc_ref)
        pad = jnp.zeros((page - ts, width), F32)
        kn = jnp.concatenate([kn_ref[...], pad], axis=0).astype(BF16)
        vn = jnp.concatenate([vn_ref[...], pad], axis=0).astype(BF16)
        s = _dot_nt(qbd_ref[...], kn)
        rr = lax.broadcasted_iota(jnp.int32, s.shape, 0)
        cc = lax.broadcasted_iota(jnp.int32, s.shape, 1)
        s = jnp.where(cc <= jnp.bitwise_and(rr, ts - 1), s, -jnp.inf)
        _softmax_step(s, lambda p: _dot(p, vn), m_ref, l_ref, acc_ref)

    qbd = qbd_ref[...]
    s = jnp.concatenate([_dot_nt(qbd, k_refs[p][...].astype(BF16)) for p in range(pp)], axis=1)

    def v_dot(pb):
        out = None
        for p in range(pp):
            d = _dot(pb[:, p * page:(p + 1) * page], v_refs[p][...].astype(BF16))
            out = d if out is None else out + d
        return out

    _softmax_step(s, v_dot, m_ref, l_ref, acc_ref)

    @pl.when(j == pl.num_programs(1) - 1)
    def _():
        lam = _diff_lambda(lam_ref, lam_init)
        n = acc_ref[...] / l_ref[...]
        outs = []
        for h in range(rows // (2 * ts)):
            blk = n[h * 2 * ts:(h + 1) * 2 * ts, h * DV:(h + 1) * DV]
            o = blk[0:ts, :] - lam * blk[ts:2 * ts, :]
            outs.append(_subln(o, sg_ref[...], lam_init))
        o_ref[...] = jnp.concatenate(outs, axis=1)


def _sattn(page_table, q, kn, vn, lamv, sg, cache_k, cache_v, layer, lam_init, pp):
    bs, ts, w = q.shape
    n_pages = page_table.shape[1]
    page = cache_k.shape[2]
    ck = cache_k.reshape(cache_k.shape[0], cache_k.shape[1], page, w)
    cv = cache_v.reshape(cache_v.shape[0], cache_v.shape[1], page, w)
    rows = (w // DK) * ts

    def page_spec(p):
        return pl.BlockSpec((None, None, page, w),
                            lambda b, j, pt: (layer, pt[b, j * pp + p], 0, 0))

    new_spec = pl.BlockSpec((None, ts, w), lambda b, j, pt: (b, 0, 0))
    grid_spec = pltpu.PrefetchScalarGridSpec(
        num_scalar_prefetch=1,
        grid=(bs, n_pages // pp),
        in_specs=[new_spec, new_spec, new_spec, _const_spec(lamv.shape), _const_spec(sg.shape)]
        + [page_spec(p) for p in range(pp)] + [page_spec(p) for p in range(pp)],
        out_specs=new_spec,
        scratch_shapes=[
            pltpu.VMEM((rows, w), BF16),
            pltpu.VMEM((rows, 1), F32), pltpu.VMEM((rows, 1), F32),
            pltpu.VMEM((rows, w), F32),
        ],
    )
    return pl.pallas_call(
        functools.partial(_sattn_kernel, pp=pp, ts=ts, lam_init=lam_init),
        grid_spec=grid_spec,
        out_shape=jax.ShapeDtypeStruct((bs, ts, w), F32),
        compiler_params=_cparams(2),
        name="sattn",
    )(page_table, q, kn, vn, lamv, sg, *([ck] * pp), *([cv] * pp))


def _post_even_kernel(x_ref, a_ref, o_ref, g_ref, wo_ref, ng_ref, shf_ref, scf_ref, gf_ref,
                      wgu_ref, wd_ref, y_ref):
    d_a = a_ref.shape[1]
    m = _dot(a_ref[...], wo_ref[0:d_a, :]) + _dot(o_ref[...].astype(BF16), wo_ref[d_a:, :])
    x1 = x_ref[...] + g_ref[...] * m
    y_ref[...] = _ffn(x1, ng_ref[...], shf_ref[...], scf_ref[...], gf_ref[...], wgu_ref, wd_ref)


def _post_even(x, a, o, g, w_out, ng, shf, scf, gf, wgu, wd, tm):
    nb, t, d = x.shape
    per_row = g.shape[1] != 1
    mod = _rows_spec(tm, d, per_row)
    return pl.pallas_call(
        _post_even_kernel,
        grid=(nb, t // tm),
        in_specs=[
            _rows_spec(tm, d), _rows_spec(tm, a.shape[2]), _rows_spec(tm, o.shape[2]), mod,
            _const_spec(w_out.shape), _const_spec(ng.shape), mod, mod, mod,
            _const_spec(wgu.shape), _const_spec(wd.shape),
        ],
        out_specs=_rows_spec(tm, d),
        out_shape=jax.ShapeDtypeStruct((nb, t, d), F32),
        compiler_params=_cparams(2),
        name="post_even",
    )(x, a, o, g, w_out, ng, shf, scf, gf, wgu, wd)


def _odd_kernel(*refs, tm, short_rows, emit_v):
    (x_ref, sh_ref, sc_ref, ng0_ref, wi_ref, bi_ref, lg_ref, lb_ref, ws_ref, bimg_ref, wo_ref,
     g_ref, ng1_ref, shf_ref, scf_ref, gf_ref, wgu_ref, wd_ref) = refs[:18]
    if emit_v:
        y_ref, v_ref, mix_ref = refs[18:]
    else:
        y_ref, mix_ref = refs[18:]
    d_c = wo_ref.shape[0]
    gw = d_c // G_C

    x = x_ref[...]
    h = _rms_mod(x, ng0_ref[...], sh_ref[...], sc_ref[...]).astype(BF16)
    u = _gelu_tanh(_dot(h, wi_ref[:, 0:d_c]) + bi_ref[:, 0:d_c])
    v = _gelu_tanh(_dot(h, wi_ref[:, d_c:]) + bi_ref[:, d_c:])
    mu = jnp.mean(v, axis=-1, keepdims=True)
    vc = v - mu
    var = jnp.mean(vc * vc, axis=-1, keepdims=True)
    vn = vc * lax.rsqrt(var + EPS) * lg_ref[...] + lb_ref[...]
    if emit_v:
        v_ref[...] = vn
    vb = vn.astype(BF16)

    r = lax.broadcasted_iota(jnp.int32, (CHUNK, CHUNK), 0)
    c = lax.broadcasted_iota(jnp.int32, (CHUNK, CHUNK), 1)
    if short_rows is None:
        for gi in range(G_C):
            wsg = jnp.where(c <= r, ws_ref[gi], 0.0).astype(BF16)
            for n in range(tm // CHUNK):
                blk = _dot(wsg, vb[n * CHUNK:(n + 1) * CHUNK, gi * gw:(gi + 1) * gw])
                mix_ref[n * CHUNK:(n + 1) * CHUNK, gi * gw:(gi + 1) * gw] = (
                    blk + bimg_ref[:, gi * gw:(gi + 1) * gw])
    else:
        rr = lax.broadcasted_iota(jnp.int32, (tm, CHUNK), 0)
        cc = lax.broadcasted_iota(jnp.int32, (tm, CHUNK), 1)
        sel = jnp.where(jnp.bitwise_and(rr, short_rows - 1) == cc, 1.0, 0.0).astype(BF16)
        r2 = lax.broadcasted_iota(jnp.int32, (tm, tm), 0)
        c2 = lax.broadcasted_iota(jnp.int32, (tm, tm), 1)
        same_seq = _shr(r2, short_rows) == _shr(c2, short_rows)
        causal = jnp.bitwise_and(c2, short_rows - 1) <= jnp.bitwise_and(r2, short_rows - 1)
        keep = jnp.where(same_seq, jnp.where(causal, 1.0, 0.0), 0.0)
        for gi in range(G_C):
            spread = _dot_nt(_dot(sel, ws_ref[gi].astype(BF16)).astype(BF16), sel)
            mg = (spread * keep).astype(BF16)
            blk = _dot(mg, vb[:, gi * gw:(gi + 1) * gw])
            mix_ref[:, gi * gw:(gi + 1) * gw] = blk + bimg_ref[:, gi * gw:(gi + 1) * gw]

    y = _dot((u * mix_ref[...]).astype(BF16), wo_ref[...])
    x1 = x + g_ref[...] * y
    y_ref[...] = _ffn(x1, ng1_ref[...], shf_ref[...], scf_ref[...], gf_ref[...], wgu_ref, wd_ref)


def _odd(x, sh, sc, ng0, w_in, b_in, ln_g, ln_b, w_s, bimg, w_out, g, ng1, shf, scf, gf, wgu, wd,
         tm, short_rows, emit_v):
    nb, t, d = x.shape
    d_c = w_out.shape[0]
    per_row = sh.shape[1] != 1
    mod = _rows_spec(tm, d, per_row)
    out_specs = [_rows_spec(tm, d)]
    out_shape = [jax.ShapeDtypeStruct((nb, t, d), F32)]
    if emit_v:
        out_specs.append(_rows_spec(tm, d_c))
        out_shape.append(jax.ShapeDtypeStruct((nb, t, d_c), F32))
    return pl.pallas_call(
        functools.partial(_odd_kernel, tm=tm, short_rows=short_rows, emit_v=emit_v),
        grid=(nb, t // tm),
        in_specs=[
            _rows_spec(tm, d), mod, mod, _const_spec(ng0.shape), _const_spec(w_in.shape),
            _const_spec(b_in.shape), _const_spec(ln_g.shape), _const_spec(ln_b.shape),
            _const_spec(w_s.shape), _const_spec(bimg.shape), _const_spec(w_out.shape),
            mod, _const_spec(ng1.shape), mod, mod, mod,
            _const_spec(wgu.shape), _const_spec(wd.shape),
        ],
        out_specs=out_specs, out_shape=out_shape,
        scratch_shapes=[pltpu.VMEM((tm, d_c), F32)],
        compiler_params=_cparams(2),
        name="odd",
    )(x, sh, sc, ng0, w_in, b_in, ln_g, ln_b, w_s, bimg, w_out, g, ng1, shf, scf, gf, wgu, wd)


def kernel(x_prompt, x_sample, c_prompt, c_sample, cache_k, cache_v, state_conv, page_table, w_ada, b_ada, norm_g, w_in_even, conv_w, q_norm_g, k_norm_g, lam_q1, lam_k1, lam_q2, lam_k2, subln_g, w_out_even, w_in_odd, b_in_odd, ln_g, ln_b, w_s, b_s, w_out_odd, w_gate_up, w_down):
    bp, tp, d = x_prompt.shape
    bs, ts, _ = x_sample.shape
    depth = w_ada.shape[0]
    d_a = conv_w.shape[2]
    qk_w = H_B * 2 * DK
    rows_s = bs * ts

    mods = _adaln(jnp.concatenate([c_prompt, c_sample], axis=0), w_ada, b_ada)

    grp = jnp.arange(qk_w) // DK
    bd = jnp.where(grp[:, None] == grp[None, :], 1.0 / DK, 0.0).astype(BF16)

    xp = x_prompt
    xs = x_sample.reshape(1, rows_s, d)
    k_p, v_p, conv_p, k_s, v_s, conv_s, chunk_v = [], [], [], [], [], [], []

    for i in range(depth):
        mp = [m.reshape(bp, 1, d) for m in jnp.split(mods[i, :bp], 6, axis=-1)]
        ms = [jnp.repeat(m, ts, axis=0).reshape(1, rows_s, d)
              for m in jnp.split(mods[i, bp:], 6, axis=-1)]
        ng0 = norm_g[i, 0].reshape(1, d)
        ng1 = norm_g[i, 1].reshape(1, d)
        wgu = w_gate_up[i].astype(BF16)
        wd = w_down[i].astype(BF16)
        if i % 2 == 0:
            e = i // 2
            lam_init = 0.8 - 0.6 * math.exp(-0.3 * i)
            w_in = w_in_even[e].astype(BF16)
            w_out = w_out_even[e].astype(BF16)
            qg = jnp.tile(q_norm_g[e], qk_w // DK).reshape(1, qk_w)
            kg = jnp.tile(k_norm_g[e], qk_w // DK).reshape(1, qk_w)
            lamv = jnp.stack([lam_q1[e], lam_k1[e], lam_q2[e], lam_k2[e]])
            sg = subln_g[e].reshape(1, DV)

            a, q, k, v, kb, vb, tail = _inproj(xp, mp[0], mp[1], ng0, w_in, conv_w[e], qg, kg, bd,
                                               None, ROW_TILE)
            o = _attn(q, kb, vb, lamv, sg, lam_init, ATTN_TILE)
            xp = _post_even(xp, a, o, mp[2], w_out, ng1, mp[3], mp[4], mp[5], wgu, wd, ROW_TILE)
            k_p.append(k.reshape(bp, tp, H_B, 2 * DK))
            v_p.append(v.reshape(bp, tp, H_B, DV))
            conv_p.append(tail[:, 8 - (CONV_W - 1):, :])

            st = state_conv[e]
            zeros = jnp.zeros((bs, ts - 1, d_a), F32)
            fill1 = jnp.concatenate([st[:, 1:2], zeros], axis=1).reshape(1, rows_s, d_a)
            fill2 = jnp.concatenate([st, zeros[:, 1:]], axis=1).reshape(1, rows_s, d_a)
            a, q, k, v, _, _, xc = _inproj(xs, ms[0], ms[1], ng0, w_in, conv_w[e], qg, kg, bd,
                                           (fill1, fill2), rows_s)
            o = _sattn(page_table, q.astype(F32).reshape(bs, ts, qk_w), k.reshape(bs, ts, qk_w),
                       v.reshape(bs, ts, -1), lamv, sg, cache_k, cache_v, e, lam_init,
                       PAGES_PER_STEP)
            xs = _post_even(xs, a, o.reshape(1, rows_s, -1), ms[2], w_out, ng1, ms[3], ms[4], ms[5],
                            wgu, wd, rows_s)
            k_s.append(k.reshape(bs, ts, H_B, 2 * DK))
            v_s.append(v.reshape(bs, ts, H_B, DV))
            conv_s.append(xc.reshape(bs, ts, d_a)[:, ts - (CONV_W - 1):, :])
        else:
            oi = i // 2
            d_c = w_out_odd.shape[1]
            w_in = w_in_odd[oi].astype(BF16)
            w_out = w_out_odd[oi].astype(BF16)
            b_in = b_in_odd[oi].reshape(1, -1)
            lg = ln_g[oi].reshape(1, d_c)
            lb = ln_b[oi].reshape(1, d_c)
            bimg = jnp.repeat(b_s[oi].T, d_c // G_C, axis=1)
            xp = _odd(xp, mp[0], mp[1], ng0, w_in, b_in, lg, lb, w_s[oi], bimg, w_out, mp[2], ng1,
                      mp[3], mp[4], mp[5], wgu, wd, ROW_TILE, None, False)[0]
            bimg_s = jnp.tile(bimg[:ts], (bs, 1))
            xs, vrows = _odd(xs, ms[0], ms[1], ng0, w_in, b_in, lg, lb, w_s[oi], bimg_s, w_out, ms[2],
                             ng1, ms[3], ms[4], ms[5], wgu, wd, rows_s, ts, True)
            chunk_v.append(vrows.reshape(bs, ts, d_c))

    return (xp, xs.reshape(bs, ts, d), jnp.stack(k_p), jnp.stack(v_p), jnp.stack(conv_p),
            jnp.stack(k_s), jnp.stack(v_s), jnp.stack(conv_s), jnp.stack(chunk_v))
```

```python
import functools
import math

import jax
import jax.numpy as jnp
from jax import lax
from jax.experimental import pallas as pl
from jax.experimental.pallas import tpu as pltpu

F32 = jnp.float32
BF16 = jnp.bfloat16
EPS = 1e-6

A_GROUPS = 8
CONV_W = 3
H_B = 4
DK = 64
DV = 2 * DK
CHUNK = 128
G_C = 8
Q_SCALE = DK ** -0.5 * math.log2(math.e)

VMEM_LIMIT_BYTES = 56 * 1024 * 1024

ROW_TILE = 512
ATTN_TILE = 512
ATTN_COLS = 512
PAGES_PER_STEP = 8


def _cparams(n_axes):
    return pltpu.CompilerParams(
        dimension_semantics=("arbitrary",) * n_axes,
        vmem_limit_bytes=VMEM_LIMIT_BYTES)


def _const_spec(shape):
    zeros = (0,) * len(shape)
    return pl.BlockSpec(shape, lambda *_: zeros, pipeline_mode=pl.Buffered(1))


def _rows_spec(tm, width, per_row=True):
    if per_row:
        return pl.BlockSpec((None, tm, width), lambda b, i: (b, i, 0))
    return pl.BlockSpec((None, 1, width), lambda b, i: (b, 0, 0))


def _dot(a, b):
    return jnp.dot(a, b, preferred_element_type=F32)


def _dot_nt(a, b):
    return lax.dot_general(a, b, (((1,), (1,)), ((), ())), preferred_element_type=F32)


def _rms_mod(x, g, shift, scale):
    ms = jnp.mean(x * x, axis=-1, keepdims=True)
    y = x * lax.rsqrt(ms + EPS) * g
    return y * (1.0 + scale) + shift


def _shr(x, pow2):
    shift = pow2.bit_length() - 1
    assert 1 << shift == pow2
    return jnp.right_shift(x, shift)


def _silu(x):
    return x / (1.0 + jnp.exp(-x))


def _gelu_tanh(x):
    c = math.sqrt(2.0 / math.pi)
    return x * (0.5 * (1.0 + jnp.tanh(c * (x + 0.044715 * (x * x * x)))))


def _ffn(x1, ng, shf, scf, gf, wgu_ref, wd_ref):
    d_ff = wd_ref.shape[0]
    n_chunks = 2 if (d_ff // 2) % 128 == 0 else 1
    ch = d_ff // n_chunks
    h = _rms_mod(x1, ng, shf, scf).astype(BF16)
    acc = None
    for c in range(n_chunks):
        g = _dot(h, wgu_ref[:, c * ch:(c + 1) * ch])
        u = _dot(h, wgu_ref[:, d_ff + c * ch:d_ff + (c + 1) * ch])
        a = (_silu(g) * u).astype(BF16)
        d = _dot(a, wd_ref[c * ch:(c + 1) * ch, :])
        acc = d if acc is None else acc + d
    return x1 + gf * acc


def _adaln_kernel(c_ref, w_ref, b_ref, o_ref):
    s = _silu(c_ref[...])
    o_ref[...] = jnp.dot(s, w_ref[...], precision=lax.Precision.HIGHEST,
                         preferred_element_type=F32) + b_ref[...]


def _adaln(c_all, w_ada, b_ada):
    depth, d, d6 = w_ada.shape
    r = c_all.shape[0]
    tn = d6 // 4
    return pl.pallas_call(
        _adaln_kernel,
        grid=(depth, d6 // tn),
        in_specs=[
            pl.BlockSpec((r, d), lambda l, j: (0, 0)),
            pl.BlockSpec((None, d, tn), lambda l, j: (l, 0, j)),
            pl.BlockSpec((None, 1, tn), lambda l, j: (l, 0, j)),
        ],
        out_specs=pl.BlockSpec((None, r, tn), lambda l, j: (l, 0, j)),
        out_shape=jax.ShapeDtypeStruct((depth, r, d6), F32),
        compiler_params=_cparams(2),
        name="adaln",
    )(c_all, w_ada, b_ada.reshape(depth, 1, d6))


def _inproj_kernel(*refs, tm, d_a, qk_w, has_fill, period):
    if has_fill:
        (x_ref, sh_ref, sc_ref, ng_ref, w_ref, cw_ref, qg_ref, kg_ref, bd_ref, f1_ref, f2_ref,
         a_ref, q_ref, k_ref, v_ref, kb_ref, vt_ref, tail_ref) = refs
    else:
        (x_ref, sh_ref, sc_ref, ng_ref, w_ref, cw_ref, qg_ref, kg_ref, bd_ref,
         a_ref, q_ref, k_ref, v_ref, kb_ref, vt_ref, tail_ref, carry_ref) = refs

    h = _rms_mod(x_ref[...], ng_ref[...], sh_ref[...], sc_ref[...]).astype(BF16)

    a_b = _dot(h, w_ref[:, 0:d_a])
    xc = _dot(h, w_ref[:, d_a:2 * d_a]) * _dot(h, w_ref[:, 2 * d_a:3 * d_a])
    row = lax.broadcasted_iota(jnp.int32, (tm, d_a), 0)
    if has_fill:
        rp = jnp.bitwise_and(row, period - 1)
        fill1 = f1_ref[...]
        fill2 = f2_ref[...]
    else:
        rp = row

        @pl.when(pl.program_id(1) == 0)
        def _():
            carry_ref[...] = jnp.zeros_like(carry_ref)

        last1 = carry_ref[pl.ds(7, 1), :]
        last2 = carry_ref[pl.ds(6, 1), :]
        fill1 = jnp.broadcast_to(last1, (tm, d_a))
        fill2 = jnp.where(row == 0, last2, last1)
    prev1 = jnp.where(rp < 1, fill1, pltpu.roll(xc, 1, 0))
    prev2 = jnp.where(rp < 2, fill2, pltpu.roll(xc, 2, 0))
    y = cw_ref[0:1, :] * prev2 + cw_ref[1:2, :] * prev1 + cw_ref[2:3, :] * xc
    a_ref[...] = (a_b * y).astype(BF16)
    tail_rows = tail_ref.shape[0]
    tail_ref[...] = xc[tm - tail_rows:tm, :]
    if not has_fill:
        carry_ref[...] = xc[tm - 8:tm, :]

    q0 = 3 * d_a
    zq = _dot(h, w_ref[:, q0:q0 + qk_w])
    msq = _dot((zq * zq).astype(BF16), bd_ref[...])
    qn = zq * lax.rsqrt(msq + EPS) * qg_ref[...]
    q_ref[...] = (qn * Q_SCALE).astype(BF16)
    zk = _dot(h, w_ref[:, q0 + qk_w:q0 + 2 * qk_w])
    msk = _dot((zk * zk).astype(BF16), bd_ref[...])
    kn = zk * lax.rsqrt(msk + EPS) * kg_ref[...]
    k_ref[...] = kn
    kb_ref[...] = kn.astype(BF16)
    zv = _dot(h, w_ref[:, q0 + 2 * qk_w:])
    v_ref[...] = zv
    vt_ref[...] = zv.T.astype(BF16)


def _inproj(x, sh, sc, ng, w_in, cw, qg, kg, bd, fills, tm):
    nb, t, d = x.shape
    d_a = cw.shape[1]
    qk_w = qg.shape[1]
    d_b = w_in.shape[1] - 3 * d_a - 2 * qk_w
    has_fill = fills is not None
    per_row = sh.shape[1] != 1
    tail_rows = tm if has_fill else 8
    in_specs = [
        _rows_spec(tm, d), _rows_spec(tm, d, per_row), _rows_spec(tm, d, per_row),
        _const_spec(ng.shape), _const_spec(w_in.shape), _const_spec(cw.shape),
        _const_spec(qg.shape), _const_spec(kg.shape), _const_spec(bd.shape),
    ]
    args = [x, sh, sc, ng, w_in, cw, qg, kg, bd]
    scratch = []
    if has_fill:
        in_specs += [_rows_spec(tm, d_a), _rows_spec(tm, d_a)]
        args += list(fills)
        tail_spec = _rows_spec(tm, d_a)
        tail_shape = (nb, t, d_a)
    else:
        scratch = [pltpu.VMEM((8, d_a), F32)]
        tail_spec = pl.BlockSpec((None, 8, d_a), lambda b, i: (b, 0, 0))
        tail_shape = (nb, 8, d_a)
    out_specs = [_rows_spec(tm, d_a), _rows_spec(tm, qk_w), _rows_spec(tm, qk_w), _rows_spec(tm, d_b),
                 _rows_spec(tm, qk_w), pl.BlockSpec((None, d_b, tm), lambda b, i: (b, 0, i)), tail_spec]
    out_shape = [
        jax.ShapeDtypeStruct((nb, t, d_a), BF16), jax.ShapeDtypeStruct((nb, t, qk_w), BF16),
        jax.ShapeDtypeStruct((nb, t, qk_w), F32), jax.ShapeDtypeStruct((nb, t, d_b), F32),
        jax.ShapeDtypeStruct((nb, t, qk_w), BF16), jax.ShapeDtypeStruct((nb, d_b, t), BF16),
        jax.ShapeDtypeStruct(tail_shape, F32),
    ]
    return pl.pallas_call(
        functools.partial(_inproj_kernel, tm=tm, d_a=d_a, qk_w=qk_w, has_fill=has_fill, period=8),
        grid=(nb, t // tm),
        in_specs=in_specs, out_specs=out_specs, out_shape=out_shape,
        scratch_shapes=scratch,
        compiler_params=_cparams(2),
        name="inproj",
    )(*args)


def _diff_lambda(lam_ref, lam_init):
    lv = lam_ref[...]
    e1 = jnp.exp(jnp.sum(lv[0:1, :] * lv[1:2, :], axis=-1, keepdims=True))
    e2 = jnp.exp(jnp.sum(lv[2:3, :] * lv[3:4, :], axis=-1, keepdims=True))
    return e1 - e2 + lam_init


def _subln(o, g, lam_init):
    ms = jnp.mean(o * o, axis=-1, keepdims=True)
    return o * lax.rsqrt(ms + EPS) * g * (1.0 - lam_init)


def _softmax_step(s, v_dot, m_ref, l_ref, acc_ref):
    m_prev = m_ref[...]
    m_new = jnp.maximum(m_prev, jnp.max(s, axis=-1, keepdims=True))
    alpha = jnp.exp2(m_prev - m_new)
    p = jnp.exp2(s - m_new)
    l_ref[...] = alpha * l_ref[...] + jnp.sum(p, axis=-1, keepdims=True)
    acc_ref[...] = alpha * acc_ref[...] + v_dot(p.astype(BF16))
    m_ref[...] = m_new


def _attn_kernel(q_ref, k_ref, vt_ref, lam_ref, sg_ref, o_ref, qs_ref, m_ref, l_ref, acc_ref,
                 *, tq, lam_init):
    i = pl.program_id(2)
    q = q_ref[...]
    lane = lax.broadcasted_iota(jnp.int32, q.shape, 1)
    zero = jnp.zeros_like(q)
    qs_ref[0:tq, :] = jnp.where(lane < DK, q, zero)
    qs_ref[tq:2 * tq, :] = jnp.where(lane >= DK, q, zero)
    m_ref[...] = jnp.full_like(m_ref, -jnp.inf)
    l_ref[...] = jnp.zeros_like(l_ref)
    acc_ref[...] = jnp.zeros_like(acc_ref)

    def step(j, diagonal):
        start = pl.multiple_of(j * tq, tq)
        k = k_ref[pl.ds(start, tq), :]
        vt = vt_ref[:, pl.ds(start, tq)]
        chains = range(0, 2 * tq, ATTN_COLS)
        sts = [_dot_nt(k, qs_ref[c0:c0 + ATTN_COLS, :]) for c0 in chains]
        for c0, st in zip(chains, sts):
            cols = slice(c0, c0 + ATTN_COLS)
            if diagonal:
                key = lax.broadcasted_iota(jnp.int32, st.shape, 0)
                qrow = lax.broadcasted_iota(jnp.int32, st.shape, 1) + (c0 % tq)
                st = jnp.where(key <= qrow, st, -jnp.inf)
            m_prev = m_ref[:, cols]
            m_new = jnp.maximum(m_prev, jnp.max(st, axis=0, keepdims=True))
            alpha = jnp.exp2(m_prev - m_new)
            pt = jnp.exp2(st - m_new)
            l_ref[:, cols] = alpha * l_ref[:, cols] + jnp.sum(pt, axis=0, keepdims=True)
            acc_ref[:, cols] = alpha * acc_ref[:, cols] + _dot(vt, pt.astype(BF16))
            m_ref[:, cols] = m_new

    def body(j, carry):
        step(j, False)
        return carry

    lax.fori_loop(0, i, body, 0)
    step(i, True)

    lam = _diff_lambda(lam_ref, lam_init)
    nt = acc_ref[...] / l_ref[...]
    ot = nt[:, 0:tq] - lam * nt[:, tq:2 * tq]
    ms = jnp.mean(ot * ot, axis=0, keepdims=True)
    ot = ot * lax.rsqrt(ms + EPS)
    o_ref[...] = (ot.T * sg_ref[...] * (1.0 - lam_init)).astype(o_ref.dtype)


def _attn(q, kb, vt, lamv, sg, lam_init, tq):
    b, t, w = q.shape
    heads = w // DV
    return pl.pallas_call(
        functools.partial(_attn_kernel, tq=tq, lam_init=lam_init),
        grid=(b, heads, t // tq),
        in_specs=[
            pl.BlockSpec((None, tq, DV), lambda b_, h, i: (b_, i, h)),
            pl.BlockSpec((None, t, DV), lambda b_, h, i: (b_, 0, h)),
            pl.BlockSpec((None, DV, t), lambda b_, h, i: (b_, h, 0)),
            _const_spec(lamv.shape), _const_spec(sg.shape),
        ],
        out_specs=pl.BlockSpec((None, tq, DV), lambda b_, h, i: (b_, i, h)),
        out_shape=jax.ShapeDtypeStruct((b, t, w), BF16),
        scratch_shapes=[
            pltpu.VMEM((2 * tq, DV), BF16),
            pltpu.VMEM((1, 2 * tq), F32), pltpu.VMEM((1, 2 * tq), F32),
            pltpu.VMEM((DV, 2 * tq), F32),
        ],
        compiler_params=_cparams(3),
        name="attn",
    )(q, kb, vt, lamv, sg)


def _sattn_kernel(pt_ref, q_ref, kn_ref, vn_ref, lam_ref, sg_ref, *rest, pp, ts, lam_init):
    k_refs = rest[0:pp]
    v_refs = rest[pp:2 * pp]
    o_ref, qs_ref, bias_ref, m_ref, l_ref, acc_ref = rest[2 * pp:]
    j = pl.program_id(1)
    rows = qs_ref.shape[0]
    hr = 2 * ts
    heads = rows // hr
    prow = k_refs[0].shape[0]

    @pl.when(j == 0)
    def _():
        q = q_ref[...]
        lane = lax.broadcasted_iota(jnp.int32, (ts, DV), 1)
        parts = []
        for h in range(heads):
            qh = q[:, h * DV:(h + 1) * DV]
            parts += [jnp.where(lane < DK, qh, 0.0), jnp.where(lane >= DK, qh, 0.0)]
        qs = jnp.concatenate(parts, axis=0).astype(BF16)
        qs_ref[...] = qs
        r = lax.broadcasted_iota(jnp.int32, (rows, prow), 0)
        c = lax.broadcasted_iota(jnp.int32, (rows, prow), 1)
        bias_ref[...] = jnp.where(jnp.bitwise_and(c, heads - 1) == _shr(r, hr), 0.0, -jnp.inf)
        m_ref[...] = jnp.full_like(m_ref, -jnp.inf)
        l_ref[...] = jnp.zeros_like(l_ref)
        acc_ref[...] = jnp.zeros_like(acc_ref)
        pad = jnp.zeros((DV - ts, DV), F32)
        kns = [jnp.concatenate([kn_ref[:, h * DV:(h + 1) * DV], pad], axis=0).astype(BF16)
               for h in range(heads)]
        vns = [jnp.concatenate([vn_ref[:, h * DV:(h + 1) * DV], pad], axis=0).astype(BF16)
               for h in range(heads)]
        s = jnp.concatenate([_dot_nt(qs[h * hr:(h + 1) * hr, :], kns[h]) for h in range(heads)],
                            axis=0)
        rr = lax.broadcasted_iota(jnp.int32, s.shape, 0)
        cc = lax.broadcasted_iota(jnp.int32, s.shape, 1)
        s = jnp.where(cc <= jnp.bitwise_and(rr, ts - 1), s, -jnp.inf)

        def new_v_dot(pb):
            return jnp.concatenate([_dot(pb[h * hr:(h + 1) * hr, :], vns[h]) for h in range(heads)],
                                   axis=0)

        _softmax_step(s, new_v_dot, m_ref, l_ref, acc_ref)

    qs = qs_ref[...]
    bias = bias_ref[...]
    s = jnp.concatenate([_dot_nt(qs, k_refs[p][...].astype(BF16)) + bias for p in range(pp)],
                        axis=1)

    def v_dot(pb):
        out = None
        for p in range(pp):
            d = _dot(pb[:, p * prow:(p + 1) * prow], v_refs[p][...].astype(BF16))
            out = d if out is None else out + d
        return out

    _softmax_step(s, v_dot, m_ref, l_ref, acc_ref)

    @pl.when(j == pl.num_programs(1) - 1)
    def _():
        lam = _diff_lambda(lam_ref, lam_init)
        n = acc_ref[...] / l_ref[...]
        outs = []
        for h in range(heads):
            o = n[h * hr:h * hr + ts, :] - lam * n[h * hr + ts:(h + 1) * hr, :]
            outs.append(_subln(o, sg_ref[...], lam_init))
        o_ref[...] = jnp.concatenate(outs, axis=1)


def _sattn(page_table, q, kn, vn, lamv, sg, cache_k, cache_v, layer, lam_init, pp):
    bs, ts, w = q.shape
    n_pages = page_table.shape[1]
    prow = cache_k.shape[2] * cache_k.shape[3]
    ck = cache_k.reshape(cache_k.shape[0], cache_k.shape[1], prow, DV)
    cv = cache_v.reshape(cache_v.shape[0], cache_v.shape[1], prow, DV)
    rows = (w // DK) * ts

    def page_spec(p):
        return pl.BlockSpec((None, None, prow, DV),
                            lambda b, j, pt: (layer, pt[b, j * pp + p], 0, 0))

    new_spec = pl.BlockSpec((None, ts, w), lambda b, j, pt: (b, 0, 0))
    grid_spec = pltpu.PrefetchScalarGridSpec(
        num_scalar_prefetch=1,
        grid=(bs, n_pages // pp),
        in_specs=[new_spec, new_spec, new_spec, _const_spec(lamv.shape), _const_spec(sg.shape)]
        + [page_spec(p) for p in range(pp)] + [page_spec(p) for p in range(pp)],
        out_specs=new_spec,
        scratch_shapes=[
            pltpu.VMEM((rows, DV), BF16),
            pltpu.VMEM((rows, prow), F32),
            pltpu.VMEM((rows, 1), F32), pltpu.VMEM((rows, 1), F32),
            pltpu.VMEM((rows, DV), F32),
        ],
    )
    return pl.pallas_call(
        functools.partial(_sattn_kernel, pp=pp, ts=ts, lam_init=lam_init),
        grid_spec=grid_spec,
        out_shape=jax.ShapeDtypeStruct((bs, ts, w), F32),
        compiler_params=_cparams(2),
        name="sattn",
    )(page_table, q, kn, vn, lamv, sg, *([ck] * pp), *([cv] * pp))


def _post_even_kernel(x_ref, a_ref, o_ref, g_ref, wo_ref, ng_ref, shf_ref, scf_ref, gf_ref,
                      wgu_ref, wd_ref, y_ref):
    d_a = a_ref.shape[1]
    m = _dot(a_ref[...], wo_ref[0:d_a, :]) + _dot(o_ref[...].astype(BF16), wo_ref[d_a:, :])
    x1 = x_ref[...] + g_ref[...] * m
    y_ref[...] = _ffn(x1, ng_ref[...], shf_ref[...], scf_ref[...], gf_ref[...], wgu_ref, wd_ref)


def _post_even(x, a, o, g, w_out, ng, shf, scf, gf, wgu, wd, tm):
    nb, t, d = x.shape
    per_row = g.shape[1] != 1
    mod = _rows_spec(tm, d, per_row)
    return pl.pallas_call(
        _post_even_kernel,
        grid=(nb, t // tm),
        in_specs=[
            _rows_spec(tm, d), _rows_spec(tm, a.shape[2]), _rows_spec(tm, o.shape[2]), mod,
            _const_spec(w_out.shape), _const_spec(ng.shape), mod, mod, mod,
            _const_spec(wgu.shape), _const_spec(wd.shape),
        ],
        out_specs=_rows_spec(tm, d),
        out_shape=jax.ShapeDtypeStruct((nb, t, d), F32),
        compiler_params=_cparams(2),
        name="post_even",
    )(x, a, o, g, w_out, ng, shf, scf, gf, wgu, wd)


def _odd_kernel(*refs, tm, short_rows, emit_v):
    (x_ref, sh_ref, sc_ref, ng0_ref, wi_ref, bi_ref, lg_ref, lb_ref, ws_ref, bimg_ref, wo_ref,
     g_ref, ng1_ref, shf_ref, scf_ref, gf_ref, wgu_ref, wd_ref) = refs[:18]
    if emit_v:
        y_ref, v_ref, mix_ref = refs[18:]
    else:
        y_ref, mix_ref = refs[18:]
    d_c = wo_ref.shape[0]
    gw = d_c // G_C

    x = x_ref[...]
    h = _rms_mod(x, ng0_ref[...], sh_ref[...], sc_ref[...]).astype(BF16)
    u = _gelu_tanh(_dot(h, wi_ref[:, 0:d_c]) + bi_ref[:, 0:d_c])
    v = _gelu_tanh(_dot(h, wi_ref[:, d_c:]) + bi_ref[:, d_c:])
    mu = jnp.mean(v, axis=-1, keepdims=True)
    vc = v - mu
    var = jnp.mean(vc * vc, axis=-1, keepdims=True)
    vn = vc * lax.rsqrt(var + EPS) * lg_ref[...] + lb_ref[...]
    if emit_v:
        v_ref[...] = vn
    vb = vn.astype(BF16)

    r = lax.broadcasted_iota(jnp.int32, (CHUNK, CHUNK), 0)
    c = lax.broadcasted_iota(jnp.int32, (CHUNK, CHUNK), 1)
    if short_rows is None:
        for gi in range(G_C):
            wsg = jnp.where(c <= r, ws_ref[gi], 0.0).astype(BF16)
            for n in range(tm // CHUNK):
                blk = _dot(wsg, vb[n * CHUNK:(n + 1) * CHUNK, gi * gw:(gi + 1) * gw])
                mix_ref[n * CHUNK:(n + 1) * CHUNK, gi * gw:(gi + 1) * gw] = (
                    blk + bimg_ref[:, gi * gw:(gi + 1) * gw])
    else:
        rr = lax.broadcasted_iota(jnp.int32, (tm, CHUNK), 0)
        cc = lax.broadcasted_iota(jnp.int32, (tm, CHUNK), 1)
        sel = jnp.where(jnp.bitwise_and(rr, short_rows - 1) == cc, 1.0, 0.0).astype(BF16)
        r2 = lax.broadcasted_iota(jnp.int32, (tm, tm), 0)
        c2 = lax.broadcasted_iota(jnp.int32, (tm, tm), 1)
        same_seq = _shr(r2, short_rows) == _shr(c2, short_rows)
        causal = jnp.bitwise_and(c2, short_rows - 1) <= jnp.bitwise_and(r2, short_rows - 1)
        keep = jnp.where(same_seq, jnp.where(causal, 1.0, 0.0), 0.0)
        for gi in range(G_C):
            spread = _dot_nt(_dot(sel, ws_ref[gi].astype(BF16)).astype(BF16), sel)
            mg = (spread * keep).astype(BF16)
            blk = _dot(mg, vb[:, gi * gw:(gi + 1) * gw])
            mix_ref[:, gi * gw:(gi + 1) * gw] = blk + bimg_ref[:, gi * gw:(gi + 1) * gw]

    y = _dot((u * mix_ref[...]).astype(BF16), wo_ref[...])
    x1 = x + g_ref[...] * y
    y_ref[...] = _ffn(x1, ng1_ref[...], shf_ref[...], scf_ref[...], gf_ref[...], wgu_ref, wd_ref)


def _odd(x, sh, sc, ng0, w_in, b_in, ln_g, ln_b, w_s, bimg, w_out, g, ng1, shf, scf, gf, wgu, wd,
         tm, short_rows, emit_v):
    nb, t, d = x.shape
    d_c = w_out.shape[0]
    per_row = sh.shape[1] != 1
    mod = _rows_spec(tm, d, per_row)
    out_specs = [_rows_spec(tm, d)]
    out_shape = [jax.ShapeDtypeStruct((nb, t, d), F32)]
    if emit_v:
        out_specs.append(_rows_spec(tm, d_c))
        out_shape.append(jax.ShapeDtypeStruct((nb, t, d_c), F32))
    return pl.pallas_call(
        functools.partial(_odd_kernel, tm=tm, short_rows=short_rows, emit_v=emit_v),
        grid=(nb, t // tm),
        in_specs=[
            _rows_spec(tm, d), mod, mod, _const_spec(ng0.shape), _const_spec(w_in.shape),
            _const_spec(b_in.shape), _const_spec(ln_g.shape), _const_spec(ln_b.shape),
            _const_spec(w_s.shape), _const_spec(bimg.shape), _const_spec(w_out.shape),
            mod, _const_spec(ng1.shape), mod, mod, mod,
            _const_spec(wgu.shape), _const_spec(wd.shape),
        ],
        out_specs=out_specs, out_shape=out_shape,
        scratch_shapes=[pltpu.VMEM((tm, d_c), F32)],
        compiler_params=_cparams(2),
        name="odd",
    )(x, sh, sc, ng0, w_in, b_in, ln_g, ln_b, w_s, bimg, w_out, g, ng1, shf, scf, gf, wgu, wd)


def kernel(x_prompt, x_sample, c_prompt, c_sample, cache_k, cache_v, state_conv, page_table, w_ada, b_ada, norm_g, w_in_even, conv_w, q_norm_g, k_norm_g, lam_q1, lam_k1, lam_q2, lam_k2, subln_g, w_out_even, w_in_odd, b_in_odd, ln_g, ln_b, w_s, b_s, w_out_odd, w_gate_up, w_down):
    bp, tp, d = x_prompt.shape
    bs, ts, _ = x_sample.shape
    depth = w_ada.shape[0]
    d_a = conv_w.shape[2]
    qk_w = H_B * 2 * DK
    rows_s = bs * ts

    mods = _adaln(jnp.concatenate([c_prompt, c_sample], axis=0), w_ada, b_ada)

    grp = jnp.arange(qk_w) // DK
    bd = jnp.where(grp[:, None] == grp[None, :], 1.0 / DK, 0.0).astype(BF16)

    xp = x_prompt
    xs = x_sample.reshape(1, rows_s, d)
    k_p, v_p, conv_p, k_s, v_s, conv_s, chunk_v = [], [], [], [], [], [], []

    for i in range(depth):
        mp = [m.reshape(bp, 1, d) for m in jnp.split(mods[i, :bp], 6, axis=-1)]
        ms = [jnp.repeat(m, ts, axis=0).reshape(1, rows_s, d)
              for m in jnp.split(mods[i, bp:], 6, axis=-1)]
        ng0 = norm_g[i, 0].reshape(1, d)
        ng1 = norm_g[i, 1].reshape(1, d)
        wgu = w_gate_up[i].astype(BF16)
        wd = w_down[i].astype(BF16)
        if i % 2 == 0:
            e = i // 2
            lam_init = 0.8 - 0.6 * math.exp(-0.3 * i)
            w_in = w_in_even[e].astype(BF16)
            w_out = w_out_even[e].astype(BF16)
            qg = jnp.tile(q_norm_g[e], qk_w // DK).reshape(1, qk_w)
            kg = jnp.tile(k_norm_g[e], qk_w // DK).reshape(1, qk_w)
            lamv = jnp.stack([lam_q1[e], lam_k1[e], lam_q2[e], lam_k2[e]])
            sg = subln_g[e].reshape(1, DV)

            a, q, k, v, kb, vt, tail = _inproj(xp, mp[0], mp[1], ng0, w_in, conv_w[e], qg, kg, bd,
                                               None, ROW_TILE)
            o = _attn(q, kb, vt, lamv, sg, lam_init, ATTN_TILE)
            xp = _post_even(xp, a, o, mp[2], w_out, ng1, mp[3], mp[4], mp[5], wgu, wd, ROW_TILE)
            k_p.append(k.reshape(bp, tp, H_B, 2 * DK))
            v_p.append(v.reshape(bp, tp, H_B, DV))
            conv_p.append(tail[:, 8 - (CONV_W - 1):, :])

            st = state_conv[e]
            zeros = jnp.zeros((bs, ts - 1, d_a), F32)
            fill1 = jnp.concatenate([st[:, 1:2], zeros], axis=1).reshape(1, rows_s, d_a)
            fill2 = jnp.concatenate([st, zeros[:, 1:]], axis=1).reshape(1, rows_s, d_a)
            a, q, k, v, _, _, xc = _inproj(xs, ms[0], ms[1], ng0, w_in, conv_w[e], qg, kg, bd,
                                           (fill1, fill2), rows_s)
            o = _sattn(page_table, q.astype(F32).reshape(bs, ts, qk_w), k.reshape(bs, ts, qk_w),
                       v.reshape(bs, ts, -1), lamv, sg, cache_k, cache_v, e, lam_init,
                       PAGES_PER_STEP)
            xs = _post_even(xs, a, o.reshape(1, rows_s, -1), ms[2], w_out, ng1, ms[3], ms[4], ms[5],
                            wgu, wd, rows_s)
            k_s.append(k.reshape(bs, ts, H_B, 2 * DK))
            v_s.append(v.reshape(bs, ts, H_B, DV))
            conv_s.append(xc.reshape(bs, ts, d_a)[:, ts - (CONV_W - 1):, :])
        else:
            oi = i // 2
            d_c = w_out_odd.shape[1]
            w_in = w_in_odd[oi].astype(BF16)
            w_out = w_out_odd[oi].astype(BF16)
            b_in = b_in_odd[oi].reshape(1, -1)
            lg = ln_g[oi].reshape(1, d_c)
            lb = ln_b[oi].reshape(1, d_c)
            bimg = jnp.repeat(b_s[oi].T, d_c // G_C, axis=1)
            xp = _odd(xp, mp[0], mp[1], ng0, w_in, b_in, lg, lb, w_s[oi], bimg, w_out, mp[2], ng1,
                      mp[3], mp[4], mp[5], wgu, wd, ROW_TILE, None, False)[0]
            bimg_s = jnp.tile(bimg[:ts], (bs, 1))
            xs, vrows = _odd(xs, ms[0], ms[1], ng0, w_in, b_in, lg, lb, w_s[oi], bimg_s, w_out, ms[2],
                             ng1, ms[3], ms[4], ms[5], wgu, wd, rows_s, ts, True)
            chunk_v.append(vrows.reshape(bs, ts, d_c))

    return (xp, xs.reshape(bs, ts, d), jnp.stack(k_p), jnp.stack(v_p), jnp.stack(conv_p),
            jnp.stack(k_s), jnp.stack(v_s), jnp.stack(conv_s), jnp.stack(chunk_v))
```

```python
import functools
import math

import jax
import jax.numpy as jnp
from jax import lax
from jax.experimental import pallas as pl
from jax.experimental.pallas import tpu as pltpu

F32 = jnp.float32
BF16 = jnp.bfloat16
EPS = 1e-6

A_GROUPS = 8
CONV_W = 3
H_B = 4
DK = 64
DV = 2 * DK
CHUNK = 128
G_C = 8
Q_SCALE = DK ** -0.5 * math.log2(math.e)

VMEM_LIMIT_BYTES = 56 * 1024 * 1024
MXU_TILE = 256

ROW_TILE = 512
ATTN_TILE = 512
ATTN_COLS = 512
PAGES_PER_STEP = 16


def _cparams(n_axes):
    return pltpu.CompilerParams(
        dimension_semantics=("arbitrary",) * n_axes,
        vmem_limit_bytes=VMEM_LIMIT_BYTES)


def _const_spec(shape):
    zeros = (0,) * len(shape)
    return pl.BlockSpec(shape, lambda *_: zeros, pipeline_mode=pl.Buffered(1))


def _rows_spec(tm, width, per_row=True):
    if per_row:
        return pl.BlockSpec((None, tm, width), lambda b, i: (b, i, 0))
    return pl.BlockSpec((None, 1, width), lambda b, i: (b, 0, 0))


def _dot(a, b):
    return jnp.dot(a, b, preferred_element_type=F32)


def _dot_nt(a, b):
    return lax.dot_general(a, b, (((1,), (1,)), ((), ())), preferred_element_type=F32)


def _rms_mod(x, g, shift, scale):
    ms = jnp.mean(x * x, axis=-1, keepdims=True)
    y = x * lax.rsqrt(ms + EPS) * g
    return y * (1.0 + scale) + shift


def _shr(x, pow2):
    shift = pow2.bit_length() - 1
    assert 1 << shift == pow2
    return jnp.right_shift(x, shift)


def _silu(x):
    return x / (1.0 + jnp.exp(-x))


def _gelu_tanh(x):
    c = math.sqrt(2.0 / math.pi)
    return x * (0.5 * (1.0 + jnp.tanh(c * (x + 0.044715 * (x * x * x)))))


def _ffn(x1, ng, shf, scf, gf, wgu_ref, wd_ref):
    d_ff = wd_ref.shape[0]
    n_tiles, rem = divmod(d_ff, MXU_TILE)
    assert rem == 0
    split = (n_tiles + 1) // 2 * MXU_TILE
    h = _rms_mod(x1, ng, shf, scf).astype(BF16)
    acc = None
    for lo, hi in ((0, split), (split, d_ff)):
        g = _dot(h, wgu_ref[:, lo:hi])
        u = _dot(h, wgu_ref[:, d_ff + lo:d_ff + hi])
        a = (_silu(g) * u).astype(BF16)
        d = _dot(a, wd_ref[lo:hi, :])
        acc = d if acc is None else acc + d
    return x1 + gf * acc


def _adaln_kernel(c_ref, w_ref, b_ref, o_ref):
    s = _silu(c_ref[...])
    o_ref[...] = jnp.dot(s, w_ref[...], precision=lax.Precision.HIGHEST,
                         preferred_element_type=F32) + b_ref[...]


def _adaln(c_all, w_ada, b_ada):
    depth, d, d6 = w_ada.shape
    r = c_all.shape[0]
    tn = d6 // 4
    return pl.pallas_call(
        _adaln_kernel,
        grid=(depth, d6 // tn),
        in_specs=[
            pl.BlockSpec((r, d), lambda l, j: (0, 0)),
            pl.BlockSpec((None, d, tn), lambda l, j: (l, 0, j)),
            pl.BlockSpec((None, 1, tn), lambda l, j: (l, 0, j)),
        ],
        out_specs=pl.BlockSpec((None, r, tn), lambda l, j: (l, 0, j)),
        out_shape=jax.ShapeDtypeStruct((depth, r, d6), F32),
        compiler_params=_cparams(2),
        name="adaln",
    )(c_all, w_ada, b_ada.reshape(depth, 1, d6))


def _inproj_kernel(*refs, tm, d_a, qk_w, has_fill, period):
    if has_fill:
        (x_ref, sh_ref, sc_ref, ng_ref, w_ref, cw_ref, qg_ref, kg_ref, bd_ref, f1_ref, f2_ref,
         a_ref, q_ref, k_ref, v_ref, kb_ref, vt_ref, tail_ref) = refs
    else:
        (x_ref, sh_ref, sc_ref, ng_ref, w_ref, cw_ref, qg_ref, kg_ref, bd_ref,
         a_ref, q_ref, k_ref, v_ref, kb_ref, vt_ref, tail_ref, carry_ref) = refs

    h = _rms_mod(x_ref[...], ng_ref[...], sh_ref[...], sc_ref[...]).astype(BF16)

    a_b = _dot(h, w_ref[:, 0:d_a])
    xc = _dot(h, w_ref[:, d_a:2 * d_a]) * _dot(h, w_ref[:, 2 * d_a:3 * d_a])
    row = lax.broadcasted_iota(jnp.int32, (tm, d_a), 0)
    if has_fill:
        rp = jnp.bitwise_and(row, period - 1)
        fill1 = f1_ref[...]
        fill2 = f2_ref[...]
    else:
        rp = row

        @pl.when(pl.program_id(1) == 0)
        def _():
            carry_ref[...] = jnp.zeros_like(carry_ref)

        last1 = carry_ref[pl.ds(7, 1), :]
        last2 = carry_ref[pl.ds(6, 1), :]
        fill1 = jnp.broadcast_to(last1, (tm, d_a))
        fill2 = jnp.where(row == 0, last2, last1)
    prev1 = jnp.where(rp < 1, fill1, pltpu.roll(xc, 1, 0))
    prev2 = jnp.where(rp < 2, fill2, pltpu.roll(xc, 2, 0))
    y = cw_ref[0:1, :] * prev2 + cw_ref[1:2, :] * prev1 + cw_ref[2:3, :] * xc
    a_ref[...] = (a_b * y).astype(BF16)
    tail_rows = tail_ref.shape[0]
    tail_ref[...] = xc[tm - tail_rows:tm, :]
    if not has_fill:
        carry_ref[...] = xc[tm - 8:tm, :]

    q0 = 3 * d_a
    zq = _dot(h, w_ref[:, q0:q0 + qk_w])
    msq = _dot((zq * zq).astype(BF16), bd_ref[...])
    qn = zq * lax.rsqrt(msq + EPS) * qg_ref[...]
    q_ref[...] = (qn * Q_SCALE).astype(BF16)
    zk = _dot(h, w_ref[:, q0 + qk_w:q0 + 2 * qk_w])
    msk = _dot((zk * zk).astype(BF16), bd_ref[...])
    kn = zk * lax.rsqrt(msk + EPS) * kg_ref[...]
    kb_ref[...] = kn.astype(BF16)
    zv = _dot(h, w_ref[:, q0 + 2 * qk_w:])
    for out_ref, val in ((k_ref, kn), (v_ref, zv)):
        if has_fill:
            out_ref[...] = val
        else:
            heads = val.shape[1] // DV
            for hh in range(heads):
                out_ref[pl.ds(hh, tm, stride=heads), :] = val[:, hh * DV:(hh + 1) * DV]
    vt_ref[...] = zv.T.astype(BF16)


def _inproj(x, sh, sc, ng, w_in, cw, qg, kg, bd, fills, tm):
    nb, t, d = x.shape
    d_a = cw.shape[1]
    qk_w = qg.shape[1]
    d_b = w_in.shape[1] - 3 * d_a - 2 * qk_w
    has_fill = fills is not None
    per_row = sh.shape[1] != 1
    tail_rows = tm if has_fill else 8
    in_specs = [
        _rows_spec(tm, d), _rows_spec(tm, d, per_row), _rows_spec(tm, d, per_row),
        _const_spec(ng.shape), _const_spec(w_in.shape), _const_spec(cw.shape),
        _const_spec(qg.shape), _const_spec(kg.shape), _const_spec(bd.shape),
    ]
    args = [x, sh, sc, ng, w_in, cw, qg, kg, bd]
    scratch = []
    if has_fill:
        in_specs += [_rows_spec(tm, d_a), _rows_spec(tm, d_a)]
        args += list(fills)
        tail_spec = _rows_spec(tm, d_a)
        tail_shape = (nb, t, d_a)
    else:
        scratch = [pltpu.VMEM((8, d_a), F32)]
        tail_spec = pl.BlockSpec((None, 8, d_a), lambda b, i: (b, 0, 0))
        tail_shape = (nb, 8, d_a)
    if has_fill:
        kv_specs = [_rows_spec(tm, qk_w), _rows_spec(tm, d_b)]
        kv_shapes = [(nb, t, qk_w), (nb, t, d_b)]
    else:
        kv_specs = [_rows_spec(tm * (qk_w // DV), DV), _rows_spec(tm * (d_b // DV), DV)]
        kv_shapes = [(nb, t * (qk_w // DV), DV), (nb, t * (d_b // DV), DV)]
    out_specs = [_rows_spec(tm, d_a), _rows_spec(tm, qk_w), *kv_specs,
                 _rows_spec(tm, qk_w), pl.BlockSpec((None, d_b, tm), lambda b, i: (b, 0, i)), tail_spec]
    out_shape = [
        jax.ShapeDtypeStruct((nb, t, d_a), BF16), jax.ShapeDtypeStruct((nb, t, qk_w), BF16),
        jax.ShapeDtypeStruct(kv_shapes[0], F32), jax.ShapeDtypeStruct(kv_shapes[1], F32),
        jax.ShapeDtypeStruct((nb, t, qk_w), BF16), jax.ShapeDtypeStruct((nb, d_b, t), BF16),
        jax.ShapeDtypeStruct(tail_shape, F32),
    ]
    return pl.pallas_call(
        functools.partial(_inproj_kernel, tm=tm, d_a=d_a, qk_w=qk_w, has_fill=has_fill, period=8),
        grid=(nb, t // tm),
        in_specs=in_specs, out_specs=out_specs, out_shape=out_shape,
        scratch_shapes=scratch,
        compiler_params=_cparams(2),
        name="inproj",
    )(*args)


def _diff_lambda(lam_ref, lam_init):
    lv = lam_ref[...]
    e1 = jnp.exp(jnp.sum(lv[0:1, :] * lv[1:2, :], axis=-1, keepdims=True))
    e2 = jnp.exp(jnp.sum(lv[2:3, :] * lv[3:4, :], axis=-1, keepdims=True))
    return e1 - e2 + lam_init


def _subln(o, g, lam_init):
    ms = jnp.mean(o * o, axis=-1, keepdims=True)
    return o * lax.rsqrt(ms + EPS) * g * (1.0 - lam_init)


def _softmax_step(s, v_dot, m_ref, l_ref, acc_ref, rows=slice(None)):
    m_prev = m_ref[rows, :]
    m_new = jnp.maximum(m_prev, jnp.max(s, axis=-1, keepdims=True))
    alpha = jnp.exp2(m_prev - m_new)
    p = jnp.exp2(s - m_new)
    l_ref[rows, :] = alpha * l_ref[rows, :] + jnp.sum(p, axis=-1, keepdims=True)
    acc_ref[rows, :] = alpha * acc_ref[rows, :] + v_dot(p.astype(BF16))
    m_ref[rows, :] = m_new


def _attn_kernel(q_ref, k_ref, vt_ref, lam_ref, sg_ref, o_ref, qs_ref, sa_ref, sb_ref, m_ref, l_ref,
                 acc_ref, *, tq, lam_init):
    i = pl.program_id(2)
    q = q_ref[...]
    lane = lax.broadcasted_iota(jnp.int32, q.shape, 1)
    zero = jnp.zeros_like(q)
    qs_ref[0:tq, :] = jnp.where(lane < DK, q, zero)
    qs_ref[tq:2 * tq, :] = jnp.where(lane >= DK, q, zero)
    m_ref[...] = jnp.full_like(m_ref, -jnp.inf)
    l_ref[...] = jnp.zeros_like(l_ref)
    acc_ref[...] = jnp.zeros_like(acc_ref)

    chains = range(0, 2 * tq, ATTN_COLS)

    def scores(j, s_ref):
        k = k_ref[pl.ds(pl.multiple_of(j * tq, tq), tq), :]
        for c0 in chains:
            s_ref[:, c0:c0 + ATTN_COLS] = _dot_nt(k, qs_ref[c0:c0 + ATTN_COLS, :])

    def step(j, s_ref, diagonal):
        vt = vt_ref[:, pl.ds(pl.multiple_of(j * tq, tq), tq)]
        for c0 in chains:
            cols = slice(c0, c0 + ATTN_COLS)
            st = s_ref[:, cols]
            if diagonal:
                key = lax.broadcasted_iota(jnp.int32, st.shape, 0)
                qrow = lax.broadcasted_iota(jnp.int32, st.shape, 1) + (c0 % tq)
                st = jnp.where(key <= qrow, st, -jnp.inf)
            m_prev = m_ref[:, cols]
            m_new = jnp.maximum(m_prev, jnp.max(st, axis=0, keepdims=True))
            alpha = jnp.exp2(m_prev - m_new)
            pt = jnp.exp2(st - m_new)
            l_ref[:, cols] = alpha * l_ref[:, cols] + jnp.sum(pt, axis=0, keepdims=True)
            acc_ref[:, cols] = alpha * acc_ref[:, cols] + _dot(vt, pt.astype(BF16))
            m_ref[:, cols] = m_new

    scores(0, sa_ref)

    def pair(jj, carry):
        j0 = 2 * jj
        scores(j0 + 1, sb_ref)
        step(j0, sa_ref, False)
        scores(j0 + 2, sa_ref)
        step(j0 + 1, sb_ref, False)
        return carry

    lax.fori_loop(0, i // 2, pair, 0)

    @pl.when(i % 2 == 0)
    def _():
        step(i, sa_ref, True)

    @pl.when(i % 2 == 1)
    def _():
        scores(i, sb_ref)
        step(i - 1, sa_ref, False)
        step(i, sb_ref, True)

    lam = _diff_lambda(lam_ref, lam_init)
    nt = acc_ref[...] / l_ref[...]
    ot = nt[:, 0:tq] - lam * nt[:, tq:2 * tq]
    ms = jnp.mean(ot * ot, axis=0, keepdims=True)
    ot = ot * lax.rsqrt(ms + EPS)
    o_ref[...] = (ot.T * sg_ref[...] * (1.0 - lam_init)).astype(o_ref.dtype)


def _attn(q, kb, vt, lamv, sg, lam_init, tq):
    b, t, w = q.shape
    heads = w // DV
    return pl.pallas_call(
        functools.partial(_attn_kernel, tq=tq, lam_init=lam_init),
        grid=(b, heads, t // tq),
        in_specs=[
            pl.BlockSpec((None, tq, DV), lambda b_, h, i: (b_, i, h)),
            pl.BlockSpec((None, t, DV), lambda b_, h, i: (b_, 0, h)),
            pl.BlockSpec((None, DV, t), lambda b_, h, i: (b_, h, 0)),
            _const_spec(lamv.shape), _const_spec(sg.shape),
        ],
        out_specs=pl.BlockSpec((None, tq, DV), lambda b_, h, i: (b_, i, h)),
        out_shape=jax.ShapeDtypeStruct((b, t, w), BF16),
        scratch_shapes=[
            pltpu.VMEM((2 * tq, DV), BF16),
            pltpu.VMEM((tq, 2 * tq), F32), pltpu.VMEM((tq, 2 * tq), F32),
            pltpu.VMEM((1, 2 * tq), F32), pltpu.VMEM((1, 2 * tq), F32),
            pltpu.VMEM((DV, 2 * tq), F32),
        ],
        compiler_params=_cparams(3),
        name="attn",
    )(q, kb, vt, lamv, sg)


def _sattn_kernel(pt_ref, q_ref, kn_ref, vn_ref, lam_ref, sg_ref, *rest, pp, ts, lam_init):
    k_refs = rest[0:pp]
    v_refs = rest[pp:2 * pp]
    o_ref, qs_ref, m_ref, l_ref, acc_ref = rest[2 * pp:]
    j = pl.program_id(1)
    rows = qs_ref.shape[0]
    hr = 2 * ts
    heads = rows // hr
    page = k_refs[0].shape[0] // heads

    @pl.when(j == 0)
    def _():
        q = q_ref[...]
        lane = lax.broadcasted_iota(jnp.int32, (ts, DV), 1)
        parts = []
        for h in range(heads):
            qh = q[:, h * DV:(h + 1) * DV]
            parts += [jnp.where(lane < DK, qh, 0.0), jnp.where(lane >= DK, qh, 0.0)]
        qs = jnp.concatenate(parts, axis=0).astype(BF16)
        qs_ref[...] = qs
        m_ref[...] = jnp.full_like(m_ref, -jnp.inf)
        l_ref[...] = jnp.zeros_like(l_ref)
        acc_ref[...] = jnp.zeros_like(acc_ref)
        pad = jnp.zeros((DV - ts, DV), F32)
        kns = [jnp.concatenate([kn_ref[:, h * DV:(h + 1) * DV], pad], axis=0).astype(BF16)
               for h in range(heads)]
        vns = [jnp.concatenate([vn_ref[:, h * DV:(h + 1) * DV], pad], axis=0).astype(BF16)
               for h in range(heads)]
        s = jnp.concatenate([_dot_nt(qs[h * hr:(h + 1) * hr, :], kns[h]) for h in range(heads)],
                            axis=0)
        rr = lax.broadcasted_iota(jnp.int32, s.shape, 0)
        cc = lax.broadcasted_iota(jnp.int32, s.shape, 1)
        s = jnp.where(cc <= jnp.bitwise_and(rr, ts - 1), s, -jnp.inf)

        def new_v_dot(pb):
            return jnp.concatenate([_dot(pb[h * hr:(h + 1) * hr, :], vns[h]) for h in range(heads)],
                                   axis=0)

        _softmax_step(s, new_v_dot, m_ref, l_ref, acc_ref)

    def head_rows(refs, h):
        return jnp.concatenate([r[pl.ds(h, page, stride=heads), :] for r in refs],
                               axis=0).astype(BF16)

    hrows = [slice(h * hr, (h + 1) * hr) for h in range(heads)]
    ss = [_dot_nt(qs_ref[hrows[h], :], head_rows(k_refs, h)) for h in range(heads)]
    for h in range(heads):
        _softmax_step(ss[h], lambda pb: _dot(pb, head_rows(v_refs, h)), m_ref, l_ref, acc_ref,
                      hrows[h])

    @pl.when(j == pl.num_programs(1) - 1)
    def _():
        lam = _diff_lambda(lam_ref, lam_init)
        n = acc_ref[...] / l_ref[...]
        outs = []
        for h in range(heads):
            o = n[h * hr:h * hr + ts, :] - lam * n[h * hr + ts:(h + 1) * hr, :]
            outs.append(_subln(o, sg_ref[...], lam_init))
        o_ref[...] = jnp.concatenate(outs, axis=1)


def _sattn(page_table, q, kn, vn, lamv, sg, cache_k, cache_v, layer, lam_init, pp):
    bs, ts, w = q.shape
    n_pages = page_table.shape[1]
    prow = cache_k.shape[2] * cache_k.shape[3]
    ck = cache_k.reshape(cache_k.shape[0], cache_k.shape[1], prow, DV)
    cv = cache_v.reshape(cache_v.shape[0], cache_v.shape[1], prow, DV)
    rows = (w // DK) * ts

    def page_spec(p):
        return pl.BlockSpec((None, None, prow, DV),
                            lambda b, j, pt: (layer, pt[b, j * pp + p], 0, 0))

    new_spec = pl.BlockSpec((None, ts, w), lambda b, j, pt: (b, 0, 0))
    grid_spec = pltpu.PrefetchScalarGridSpec(
        num_scalar_prefetch=1,
        grid=(bs, n_pages // pp),
        in_specs=[new_spec, new_spec, new_spec, _const_spec(lamv.shape), _const_spec(sg.shape)]
        + [page_spec(p) for p in range(pp)] + [page_spec(p) for p in range(pp)],
        out_specs=new_spec,
        scratch_shapes=[
            pltpu.VMEM((rows, DV), BF16),
            pltpu.VMEM((rows, 1), F32), pltpu.VMEM((rows, 1), F32),
            pltpu.VMEM((rows, DV), F32),
        ],
    )
    return pl.pallas_call(
        functools.partial(_sattn_kernel, pp=pp, ts=ts, lam_init=lam_init),
        grid_spec=grid_spec,
        out_shape=jax.ShapeDtypeStruct((bs, ts, w), F32),
        compiler_params=_cparams(2),
        name="sattn",
    )(page_table, q, kn, vn, lamv, sg, *([ck] * pp), *([cv] * pp))


def _post_even_kernel(x_ref, a_ref, o_ref, g_ref, wo_ref, ng_ref, shf_ref, scf_ref, gf_ref,
                      wgu_ref, wd_ref, y_ref):
    d_a = a_ref.shape[1]
    m = _dot(a_ref[...], wo_ref[0:d_a, :]) + _dot(o_ref[...].astype(BF16), wo_ref[d_a:, :])
    x1 = x_ref[...] + g_ref[...] * m
    y_ref[...] = _ffn(x1, ng_ref[...], shf_ref[...], scf_ref[...], gf_ref[...], wgu_ref, wd_ref)


def _post_even(x, a, o, g, w_out, ng, shf, scf, gf, wgu, wd, tm):
    nb, t, d = x.shape
    per_row = g.shape[1] != 1
    mod = _rows_spec(tm, d, per_row)
    return pl.pallas_call(
        _post_even_kernel,
        grid=(nb, t // tm),
        in_specs=[
            _rows_spec(tm, d), _rows_spec(tm, a.shape[2]), _rows_spec(tm, o.shape[2]), mod,
            _const_spec(w_out.shape), _const_spec(ng.shape), mod, mod, mod,
            _const_spec(wgu.shape), _const_spec(wd.shape),
        ],
        out_specs=_rows_spec(tm, d),
        out_shape=jax.ShapeDtypeStruct((nb, t, d), F32),
        compiler_params=_cparams(2),
        name="post_even",
    )(x, a, o, g, w_out, ng, shf, scf, gf, wgu, wd)


def _odd_kernel(*refs, tm, short_rows, emit_v):
    (x_ref, sh_ref, sc_ref, ng0_ref, wi_ref, bi_ref, lg_ref, lb_ref, ws_ref, bimg_ref, wo_ref,
     g_ref, ng1_ref, shf_ref, scf_ref, gf_ref, wgu_ref, wd_ref) = refs[:18]
    if emit_v:
        y_ref, v_ref, mix_ref = refs[18:]
    else:
        y_ref, mix_ref = refs[18:]
    d_c = wo_ref.shape[0]
    gw = d_c // G_C

    x = x_ref[...]
    h = _rms_mod(x, ng0_ref[...], sh_ref[...], sc_ref[...]).astype(BF16)
    u = _gelu_tanh(_dot(h, wi_ref[:, 0:d_c]) + bi_ref[:, 0:d_c])
    v = _gelu_tanh(_dot(h, wi_ref[:, d_c:]) + bi_ref[:, d_c:])
    mu = jnp.mean(v, axis=-1, keepdims=True)
    vc = v - mu
    var = jnp.mean(vc * vc, axis=-1, keepdims=True)
    vn = vc * lax.rsqrt(var + EPS) * lg_ref[...] + lb_ref[...]
    if emit_v:
        v_ref[...] = vn
    vb = vn.astype(BF16)

    r = lax.broadcasted_iota(jnp.int32, (CHUNK, CHUNK), 0)
    c = lax.broadcasted_iota(jnp.int32, (CHUNK, CHUNK), 1)
    if short_rows is None:
        for gi in range(G_C):
            wsg = jnp.where(c <= r, ws_ref[gi], 0.0).astype(BF16)
            for n in range(tm // CHUNK):
                blk = _dot(wsg, vb[n * CHUNK:(n + 1) * CHUNK, gi * gw:(gi + 1) * gw])
                mix_ref[n * CHUNK:(n + 1) * CHUNK, gi * gw:(gi + 1) * gw] = (
                    blk + bimg_ref[:, gi * gw:(gi + 1) * gw])
    else:
        rr = lax.broadcasted_iota(jnp.int32, (tm, CHUNK), 0)
        cc = lax.broadcasted_iota(jnp.int32, (tm, CHUNK), 1)
        sel = jnp.where(jnp.bitwise_and(rr, short_rows - 1) == cc, 1.0, 0.0).astype(BF16)
        r2 = lax.broadcasted_iota(jnp.int32, (tm, tm), 0)
        c2 = lax.broadcasted_iota(jnp.int32, (tm, tm), 1)
        same_seq = _shr(r2, short_rows) == _shr(c2, short_rows)
        causal = jnp.bitwise_and(c2, short_rows - 1) <= jnp.bitwise_and(r2, short_rows - 1)
        keep = jnp.where(same_seq, jnp.where(causal, 1.0, 0.0), 0.0)
        for gi in range(G_C):
            spread = _dot_nt(_dot(sel, ws_ref[gi].astype(BF16)).astype(BF16), sel)
            mg = (spread * keep).astype(BF16)
            blk = _dot(mg, vb[:, gi * gw:(gi + 1) * gw])
            mix_ref[:, gi * gw:(gi + 1) * gw] = blk + bimg_ref[:, gi * gw:(gi + 1) * gw]

    y = _dot((u * mix_ref[...]).astype(BF16), wo_ref[...])
    x1 = x + g_ref[...] * y
    y_ref[...] = _ffn(x1, ng1_ref[...], shf_ref[...], scf_ref[...], gf_ref[...], wgu_ref, wd_ref)


def _odd(x, sh, sc, ng0, w_in, b_in, ln_g, ln_b, w_s, bimg, w_out, g, ng1, shf, scf, gf, wgu, wd,
         tm, short_rows, emit_v):
    nb, t, d = x.shape
    d_c = w_out.shape[0]
    per_row = sh.shape[1] != 1
    mod = _rows_spec(tm, d, per_row)
    out_specs = [_rows_spec(tm, d)]
    out_shape = [jax.ShapeDtypeStruct((nb, t, d), F32)]
    if emit_v:
        out_specs.append(_rows_spec(tm, d_c))
        out_shape.append(jax.ShapeDtypeStruct((nb, t, d_c), F32))
    return pl.pallas_call(
        functools.partial(_odd_kernel, tm=tm, short_rows=short_rows, emit_v=emit_v),
        grid=(nb, t // tm),
        in_specs=[
            _rows_spec(tm, d), mod, mod, _const_spec(ng0.shape), _const_spec(w_in.shape),
            _const_spec(b_in.shape), _const_spec(ln_g.shape), _const_spec(ln_b.shape),
            _const_spec(w_s.shape), _const_spec(bimg.shape), _const_spec(w_out.shape),
            mod, _const_spec(ng1.shape), mod, mod, mod,
            _const_spec(wgu.shape), _const_spec(wd.shape),
        ],
        out_specs=out_specs, out_shape=out_shape,
        scratch_shapes=[pltpu.VMEM((tm, d_c), F32)],
        compiler_params=_cparams(2),
        name="odd",
    )(x, sh, sc, ng0, w_in, b_in, ln_g, ln_b, w_s, bimg, w_out, g, ng1, shf, scf, gf, wgu, wd)


def kernel(x_prompt, x_sample, c_prompt, c_sample, cache_k, cache_v, state_conv, page_table, w_ada, b_ada, norm_g, w_in_even, conv_w, q_norm_g, k_norm_g, lam_q1, lam_k1, lam_q2, lam_k2, subln_g, w_out_even, w_in_odd, b_in_odd, ln_g, ln_b, w_s, b_s, w_out_odd, w_gate_up, w_down):
    bp, tp, d = x_prompt.shape
    bs, ts, _ = x_sample.shape
    depth = w_ada.shape[0]
    d_a = conv_w.shape[2]
    qk_w = H_B * 2 * DK
    rows_s = bs * ts

    mods = _adaln(jnp.concatenate([c_prompt, c_sample], axis=0), w_ada, b_ada)

    grp = jnp.arange(qk_w) // DK
    bd = jnp.where(grp[:, None] == grp[None, :], 1.0 / DK, 0.0).astype(BF16)

    xp = x_prompt
    xs = x_sample.reshape(1, rows_s, d)
    k_p, v_p, conv_p, k_s, v_s, conv_s, chunk_v = [], [], [], [], [], [], []

    for i in range(depth):
        mp = [m.reshape(bp, 1, d) for m in jnp.split(mods[i, :bp], 6, axis=-1)]
        ms = [jnp.repeat(m, ts, axis=0).reshape(1, rows_s, d)
              for m in jnp.split(mods[i, bp:], 6, axis=-1)]
        ng0 = norm_g[i, 0].reshape(1, d)
        ng1 = norm_g[i, 1].reshape(1, d)
        wgu = w_gate_up[i].astype(BF16)
        wd = w_down[i].astype(BF16)
        if i % 2 == 0:
            e = i // 2
            lam_init = 0.8 - 0.6 * math.exp(-0.3 * i)
            w_in = w_in_even[e].astype(BF16)
            w_out = w_out_even[e].astype(BF16)
            qg = jnp.tile(q_norm_g[e], qk_w // DK).reshape(1, qk_w)
            kg = jnp.tile(k_norm_g[e], qk_w // DK).reshape(1, qk_w)
            lamv = jnp.stack([lam_q1[e], lam_k1[e], lam_q2[e], lam_k2[e]])
            sg = subln_g[e].reshape(1, DV)

            a, q, k, v, kb, vt, tail = _inproj(xp, mp[0], mp[1], ng0, w_in, conv_w[e], qg, kg, bd,
                                               None, ROW_TILE)
            o = _attn(q, kb, vt, lamv, sg, lam_init, ATTN_TILE)
            xp = _post_even(xp, a, o, mp[2], w_out, ng1, mp[3], mp[4], mp[5], wgu, wd, ROW_TILE)
            k_p.append(k.reshape(bp, tp, H_B, 2 * DK))
            v_p.append(v.reshape(bp, tp, H_B, DV))
            conv_p.append(tail[:, 8 - (CONV_W - 1):, :])

            st = state_conv[e]
            zeros = jnp.zeros((bs, ts - 1, d_a), F32)
            fill1 = jnp.concatenate([st[:, 1:2], zeros], axis=1).reshape(1, rows_s, d_a)
            fill2 = jnp.concatenate([st, zeros[:, 1:]], axis=1).reshape(1, rows_s, d_a)
            a, q, k, v, _, _, xc = _inproj(xs, ms[0], ms[1], ng0, w_in, conv_w[e], qg, kg, bd,
                                           (fill1, fill2), rows_s)
            o = _sattn(page_table, q.astype(F32).reshape(bs, ts, qk_w), k.reshape(bs, ts, qk_w),
                       v.reshape(bs, ts, -1), lamv, sg, cache_k, cache_v, e, lam_init,
                       PAGES_PER_STEP)
            xs = _post_even(xs, a, o.reshape(1, rows_s, -1), ms[2], w_out, ng1, ms[3], ms[4], ms[5],
                            wgu, wd, rows_s)
            k_s.append(k.reshape(bs, ts, H_B, 2 * DK))
            v_s.append(v.reshape(bs, ts, H_B, DV))
            conv_s.append(xc.reshape(bs, ts, d_a)[:, ts - (CONV_W - 1):, :])
        else:
            oi = i // 2
            d_c = w_out_odd.shape[1]
            w_in = w_in_odd[oi].astype(BF16)
            w_out = w_out_odd[oi].astype(BF16)
            b_in = b_in_odd[oi].reshape(1, -1)
            lg = ln_g[oi].reshape(1, d_c)
            lb = ln_b[oi].reshape(1, d_c)
            bimg = jnp.repeat(b_s[oi].T, d_c // G_C, axis=1)
            xp = _odd(xp, mp[0], mp[1], ng0, w_in, b_in, lg, lb, w_s[oi], bimg, w_out, mp[2], ng1,
                      mp[3], mp[4], mp[5], wgu, wd, ROW_TILE, None, False)[0]
            bimg_s = jnp.tile(bimg[:ts], (bs, 1))
            xs, vrows = _odd(xs, ms[0], ms[1], ng0, w_in, b_in, lg, lb, w_s[oi], bimg_s, w_out, ms[2],
                             ng1, ms[3], ms[4], ms[5], wgu, wd, rows_s, ts, True)
            chunk_v.append(vrows.reshape(bs, ts, d_c))

    return (xp, xs.reshape(bs, ts, d), jnp.stack(k_p), jnp.stack(v_p), jnp.stack(conv_p),
            jnp.stack(k_s), jnp.stack(v_s), jnp.stack(conv_s), jnp.stack(chunk_v))
```

```python
import functools
import math
from typing import NamedTuple

import jax
import jax.numpy as jnp
from jax import lax
from jax.experimental import pallas as pl
from jax.experimental.pallas import tpu as pltpu

F32 = jnp.float32
BF16 = jnp.bfloat16
EPS = 1e-6

A_GROUPS = 8
CONV_W = 3
H_B = 4
DK = 64
DV = 2 * DK
CHUNK = 128
G_C = 8
Q_SCALE = DK ** -0.5 * math.log2(math.e)

VMEM_LIMIT_BYTES = 56 * 1024 * 1024
MXU_TILE = 256
BF16_ROWS = 16

ROW_TILE = 512
ATTN_TILE = 512
ATTN_COLS = 512
PAGES_PER_STEP = 32


def _cparams(n_axes):
    return pltpu.CompilerParams(
        dimension_semantics=("arbitrary",) * n_axes,
        vmem_limit_bytes=VMEM_LIMIT_BYTES)


def _const_spec(shape):
    zeros = (0,) * len(shape)
    return pl.BlockSpec(shape, lambda *_: zeros, pipeline_mode=pl.Buffered(1))


class _LayerOf(NamedTuple):
    stack: jax.Array
    layer: int


def _layer_spec(w):
    shape = w.stack.shape
    index = (w.layer,) + (0,) * (len(shape) - 1)
    return pl.BlockSpec((None,) + shape[1:], lambda *_: index, pipeline_mode=pl.Buffered(1))


def _rows_spec(tm, width, per_row=True):
    if per_row:
        return pl.BlockSpec((None, tm, width), lambda b, i: (b, i, 0))
    return pl.BlockSpec((None, 1, width), lambda b, i: (b, 0, 0))


def _dot(a, b):
    return jnp.dot(a, b, preferred_element_type=F32)


def _dot_nt(a, b):
    return lax.dot_general(a, b, (((1,), (1,)), ((), ())), preferred_element_type=F32)


def _rms_mod(x, g, shift, scale):
    ms = jnp.mean(x * x, axis=-1, keepdims=True)
    y = x * lax.rsqrt(ms + EPS) * g
    return y * (1.0 + scale) + shift


def _shr(x, pow2):
    shift = pow2.bit_length() - 1
    assert 1 << shift == pow2
    return jnp.right_shift(x, shift)


def _silu(x):
    return x / (1.0 + jnp.exp(-x))


def _gelu_tanh(x):
    c = math.sqrt(2.0 / math.pi)
    half = 0.5 * x
    return half + half * jnp.tanh(x * (c + (c * 0.044715) * (x * x)))


def _ffn(x1, ng, shf, scf, gf, wgu_ref, wd_ref):
    d_ff = wd_ref.shape[0]
    n_tiles, rem = divmod(d_ff, MXU_TILE)
    assert rem == 0
    split = (n_tiles + 1) // 2 * MXU_TILE
    h = _rms_mod(x1, ng, shf, scf).astype(BF16)
    acc = None
    for lo, hi in ((0, split), (split, d_ff)):
        g = _dot(h, wgu_ref[:, lo:hi])
        u = _dot(h, wgu_ref[:, d_ff + lo:d_ff + hi])
        a = (_silu(g) * u).astype(BF16)
        d = _dot(a, wd_ref[lo:hi, :])
        acc = d if acc is None else acc + d
    return x1 + gf * acc


def _adaln_kernel(c_ref, w_ref, b_ref, o_ref):
    s = _silu(c_ref[...])
    o_ref[...] = jnp.dot(s, w_ref[...], precision=lax.Precision.HIGHEST,
                         preferred_element_type=F32) + b_ref[...]


def _adaln(c_all, w_ada, b_ada):
    depth, d, d6 = w_ada.shape
    r = c_all.shape[0]
    tn = d6 // 4
    return pl.pallas_call(
        _adaln_kernel,
        grid=(depth, d6 // tn),
        in_specs=[
            pl.BlockSpec((r, d), lambda l, j: (0, 0)),
            pl.BlockSpec((None, d, tn), lambda l, j: (l, 0, j)),
            pl.BlockSpec((None, 1, tn), lambda l, j: (l, 0, j)),
        ],
        out_specs=pl.BlockSpec((None, r, tn), lambda l, j: (l, 0, j)),
        out_shape=jax.ShapeDtypeStruct((depth, r, d6), F32),
        compiler_params=_cparams(2),
        name="adaln",
    )(c_all, w_ada, b_ada.reshape(depth, 1, d6))


def _inproj_kernel(*refs, tm, d_a, qk_w, has_fill, period):
    if has_fill:
        (x_ref, sh_ref, sc_ref, ng_ref, w_ref, cw_ref, qg_ref, kg_ref, bd_ref, f1_ref, f2_ref,
         a_ref, q_ref, k_ref, v_ref, kb_ref, vt_ref, tail_ref) = refs
    else:
        (x_ref, sh_ref, sc_ref, ng_ref, w_ref, cw_ref, qg_ref, kg_ref, bd_ref,
         a_ref, q_ref, k_ref, v_ref, kb_ref, vt_ref, tail_ref, carry_ref) = refs

    h = _rms_mod(x_ref[...], ng_ref[...], sh_ref[...], sc_ref[...]).astype(BF16)

    a_b = _dot(h, w_ref[:, 0:d_a])
    xc = _dot(h, w_ref[:, d_a:2 * d_a]) * _dot(h, w_ref[:, 2 * d_a:3 * d_a])
    row = lax.broadcasted_iota(jnp.int32, (tm, d_a), 0)
    if has_fill:
        rp = jnp.bitwise_and(row, period - 1)
        fill1 = f1_ref[...]
        fill2 = f2_ref[...]
    else:
        rp = row

        @pl.when(pl.program_id(1) == 0)
        def _():
            carry_ref[...] = jnp.zeros_like(carry_ref)

        last1 = carry_ref[pl.ds(7, 1), :]
        last2 = carry_ref[pl.ds(6, 1), :]
        fill1 = jnp.broadcast_to(last1, (tm, d_a))
        fill2 = jnp.where(row == 0, last2, last1)
    prev1 = jnp.where(rp < 1, fill1, pltpu.roll(xc, 1, 0))
    prev2 = jnp.where(rp < 2, fill2, pltpu.roll(xc, 2, 0))
    y = cw_ref[0:1, :] * prev2 + cw_ref[1:2, :] * prev1 + cw_ref[2:3, :] * xc
    a_ref[...] = (a_b * y).astype(BF16)
    tail_rows = tail_ref.shape[0]
    tail_ref[...] = xc[tm - tail_rows:tm, :]
    if not has_fill:
        carry_ref[...] = xc[tm - 8:tm, :]

    q0 = 3 * d_a
    zq = _dot(h, w_ref[:, q0:q0 + qk_w])
    msq = _dot((zq * zq).astype(BF16), bd_ref[...])
    qn = zq * lax.rsqrt(msq + EPS) * qg_ref[...]
    q_ref[...] = (qn * Q_SCALE).astype(BF16)
    zk = _dot(h, w_ref[:, q0 + qk_w:q0 + 2 * qk_w])
    msk = _dot((zk * zk).astype(BF16), bd_ref[...])
    kn = zk * lax.rsqrt(msk + EPS) * kg_ref[...]
    kb_ref[...] = kn.astype(BF16)
    zv = _dot(h, w_ref[:, q0 + 2 * qk_w:])
    for out_ref, val in ((k_ref, kn), (v_ref, zv)):
        if has_fill:
            out_ref[...] = val
        else:
            heads = val.shape[1] // DV
            for hh in range(heads):
                out_ref[pl.ds(hh, tm, stride=heads), :] = val[:, hh * DV:(hh + 1) * DV]
    vt_ref[...] = zv.T.astype(BF16)


def _inproj(x, sh, sc, ng, w_in, cw, qg, kg, bd, fills, tm):
    nb, t, d = x.shape
    d_a = cw.shape[1]
    qk_w = qg.shape[1]
    d_b = w_in.shape[1] - 3 * d_a - 2 * qk_w
    has_fill = fills is not None
    per_row = sh.shape[1] != 1
    tail_rows = tm if has_fill else 8
    in_specs = [
        _rows_spec(tm, d), _rows_spec(tm, d, per_row), _rows_spec(tm, d, per_row),
        _const_spec(ng.shape), _const_spec(w_in.shape), _const_spec(cw.shape),
        _const_spec(qg.shape), _const_spec(kg.shape), _const_spec(bd.shape),
    ]
    args = [x, sh, sc, ng, w_in, cw, qg, kg, bd]
    scratch = []
    if has_fill:
        in_specs += [_rows_spec(tm, d_a), _rows_spec(tm, d_a)]
        args += list(fills)
        tail_spec = _rows_spec(tm, d_a)
        tail_shape = (nb, t, d_a)
    else:
        scratch = [pltpu.VMEM((8, d_a), F32)]
        tail_spec = pl.BlockSpec((None, 8, d_a), lambda b, i: (b, 0, 0))
        tail_shape = (nb, 8, d_a)
    if has_fill:
        kv_specs = [_rows_spec(tm, qk_w), _rows_spec(tm, d_b)]
        kv_shapes = [(nb, t, qk_w), (nb, t, d_b)]
    else:
        kv_specs = [_rows_spec(tm * (qk_w // DV), DV), _rows_spec(tm * (d_b // DV), DV)]
        kv_shapes = [(nb, t * (qk_w // DV), DV), (nb, t * (d_b // DV), DV)]
    out_specs = [_rows_spec(tm, d_a), _rows_spec(tm, qk_w), *kv_specs,
                 _rows_spec(tm, qk_w), pl.BlockSpec((None, d_b, tm), lambda b, i: (b, 0, i)), tail_spec]
    out_shape = [
        jax.ShapeDtypeStruct((nb, t, d_a), BF16), jax.ShapeDtypeStruct((nb, t, qk_w), BF16),
        jax.ShapeDtypeStruct(kv_shapes[0], F32), jax.ShapeDtypeStruct(kv_shapes[1], F32),
        jax.ShapeDtypeStruct((nb, t, qk_w), BF16), jax.ShapeDtypeStruct((nb, d_b, t), BF16),
        jax.ShapeDtypeStruct(tail_shape, F32),
    ]
    return pl.pallas_call(
        functools.partial(_inproj_kernel, tm=tm, d_a=d_a, qk_w=qk_w, has_fill=has_fill, period=8),
        grid=(nb, t // tm),
        in_specs=in_specs, out_specs=out_specs, out_shape=out_shape,
        scratch_shapes=scratch,
        compiler_params=_cparams(2),
        name="inproj",
    )(*args)


def _diff_lambda(lam_ref, lam_init):
    lv = lam_ref[...]
    e1 = jnp.exp(jnp.sum(lv[0:1, :] * lv[1:2, :], axis=-1, keepdims=True))
    e2 = jnp.exp(jnp.sum(lv[2:3, :] * lv[3:4, :], axis=-1, keepdims=True))
    return e1 - e2 + lam_init


def _subln(o, g, lam_init):
    ms = jnp.mean(o * o, axis=-1, keepdims=True)
    return o * lax.rsqrt(ms + EPS) * g * (1.0 - lam_init)


def _softmax_step(s, v_dot, m_ref, l_ref, acc_ref, rows=slice(None)):
    m_prev = m_ref[rows, :]
    m_new = jnp.maximum(m_prev, jnp.max(s, axis=-1, keepdims=True))
    alpha = jnp.exp2(m_prev - m_new)
    p = jnp.exp2(s - m_new)
    l_ref[rows, :] = alpha * l_ref[rows, :] + jnp.sum(p, axis=-1, keepdims=True)
    acc_ref[rows, :] = alpha * acc_ref[rows, :] + v_dot(p.astype(BF16))
    m_ref[rows, :] = m_new


def _attn_kernel(q_ref, k_ref, vt_ref, lam_ref, sg_ref, o_ref, qs_ref, sa_ref, sb_ref, m_ref, acc_ref,
                 *, tq, lam_init):
    i = pl.program_id(2)
    q = q_ref[...]
    lane = lax.broadcasted_iota(jnp.int32, q.shape, 1)
    zero = jnp.zeros_like(q)
    qs_ref[0:tq, :] = jnp.where(lane < DK, q, zero)
    qs_ref[tq:2 * tq, :] = jnp.where(lane >= DK, q, zero)
    m_ref[...] = jnp.full_like(m_ref, -jnp.inf)
    acc_ref[...] = jnp.zeros_like(acc_ref)

    chains = range(0, 2 * tq, ATTN_COLS)
    ones = jnp.ones((acc_ref.shape[0] - DV, tq), BF16)

    def scores(j, s_ref):
        k = k_ref[pl.ds(pl.multiple_of(j * tq, tq), tq), :]
        for c0 in chains:
            s_ref[:, c0:c0 + ATTN_COLS] = _dot_nt(k, qs_ref[c0:c0 + ATTN_COLS, :])

    def step(j, s_ref, diagonal):
        vt = vt_ref[:, pl.ds(pl.multiple_of(j * tq, tq), tq)]
        vt = jnp.concatenate([vt, ones], axis=0)
        for c0 in chains:
            cols = slice(c0, c0 + ATTN_COLS)
            st = s_ref[:, cols]
            if diagonal:
                key = lax.broadcasted_iota(jnp.int32, st.shape, 0)
                qrow = lax.broadcasted_iota(jnp.int32, st.shape, 1) + (c0 % tq)
                st = jnp.where(key <= qrow, st, -jnp.inf)
            m_prev = m_ref[:, cols]
            m_new = jnp.maximum(m_prev, jnp.max(st, axis=0, keepdims=True))
            alpha = jnp.exp2(m_prev - m_new)
            pt = jnp.exp2(st - m_new)
            acc_ref[:, cols] = alpha * acc_ref[:, cols] + _dot(vt, pt.astype(BF16))
            m_ref[:, cols] = m_new

    scores(0, sa_ref)

    def pair(jj, carry):
        j0 = 2 * jj
        scores(j0 + 1, sb_ref)
        step(j0, sa_ref, False)
        scores(j0 + 2, sa_ref)
        step(j0 + 1, sb_ref, False)
        return carry

    lax.fori_loop(0, i // 2, pair, 0)

    @pl.when(i % 2 == 0)
    def _():
        step(i, sa_ref, True)

    @pl.when(i % 2 == 1)
    def _():
        scores(i, sb_ref)
        step(i - 1, sa_ref, False)
        step(i, sb_ref, True)

    lam = _diff_lambda(lam_ref, lam_init)
    nt = acc_ref[0:DV, :] / acc_ref[DV:DV + 1, :]
    ot = nt[:, 0:tq] - lam * nt[:, tq:2 * tq]
    ms = jnp.mean(ot * ot, axis=0, keepdims=True)
    ot = ot * lax.rsqrt(ms + EPS)
    o_ref[...] = (ot.T * sg_ref[...] * (1.0 - lam_init)).astype(o_ref.dtype)


def _attn(q, kb, vt, lamv, sg, lam_init, tq):
    b, t, w = q.shape
    heads = w // DV
    return pl.pallas_call(
        functools.partial(_attn_kernel, tq=tq, lam_init=lam_init),
        grid=(b, heads, t // tq),
        in_specs=[
            pl.BlockSpec((None, tq, DV), lambda b_, h, i: (b_, i, h)),
            pl.BlockSpec((None, t, DV), lambda b_, h, i: (b_, 0, h)),
            pl.BlockSpec((None, DV, t), lambda b_, h, i: (b_, h, 0)),
            _const_spec(lamv.shape), _const_spec(sg.shape),
        ],
        out_specs=pl.BlockSpec((None, tq, DV), lambda b_, h, i: (b_, i, h)),
        out_shape=jax.ShapeDtypeStruct((b, t, w), BF16),
        scratch_shapes=[
            pltpu.VMEM((2 * tq, DV), BF16),
            pltpu.VMEM((tq, 2 * tq), F32), pltpu.VMEM((tq, 2 * tq), F32),
            pltpu.VMEM((1, 2 * tq), F32),
            pltpu.VMEM((DV + BF16_ROWS, 2 * tq), F32),
        ],
        compiler_params=_cparams(3),
        name="attn",
    )(q, kb, vt, lamv, sg)


def _sattn_kernel(pt_ref, q_ref, kn_ref, vn_ref, lam_ref, sg_ref, *rest, pp, ts, lam_init):
    k_refs = rest[0:pp]
    v_refs = rest[pp:2 * pp]
    o_ref, qs_ref, m_ref, l_ref, acc_ref = rest[2 * pp:]
    j = pl.program_id(1)
    rows = qs_ref.shape[0]
    hr = 2 * ts
    heads = rows // hr
    page = k_refs[0].shape[0] // heads

    @pl.when(j == 0)
    def _():
        q = q_ref[...]
        lane = lax.broadcasted_iota(jnp.int32, (ts, DV), 1)
        parts = []
        for h in range(heads):
            qh = q[:, h * DV:(h + 1) * DV]
            parts += [jnp.where(lane < DK, qh, 0.0), jnp.where(lane >= DK, qh, 0.0)]
        qs = jnp.concatenate(parts, axis=0).astype(BF16)
        qs_ref[...] = qs
        m_ref[...] = jnp.full_like(m_ref, -jnp.inf)
        l_ref[...] = jnp.zeros_like(l_ref)
        acc_ref[...] = jnp.zeros_like(acc_ref)
        pad = jnp.zeros((DV - ts, DV), F32)
        kns = [jnp.concatenate([kn_ref[:, h * DV:(h + 1) * DV], pad], axis=0).astype(BF16)
               for h in range(heads)]
        vns = [jnp.concatenate([vn_ref[:, h * DV:(h + 1) * DV], pad], axis=0).astype(BF16)
               for h in range(heads)]
        s = jnp.concatenate([_dot_nt(qs[h * hr:(h + 1) * hr, :], kns[h]) for h in range(heads)],
                            axis=0)
        rr = lax.broadcasted_iota(jnp.int32, s.shape, 0)
        cc = lax.broadcasted_iota(jnp.int32, s.shape, 1)
        s = jnp.where(cc <= jnp.bitwise_and(rr, ts - 1), s, -jnp.inf)

        def new_v_dot(pb):
            return jnp.concatenate([_dot(pb[h * hr:(h + 1) * hr, :], vns[h]) for h in range(heads)],
                                   axis=0)

        _softmax_step(s, new_v_dot, m_ref, l_ref, acc_ref)

    def head_rows(refs, h):
        return jnp.concatenate([r[pl.ds(h, page, stride=heads), :] for r in refs],
                               axis=0).astype(BF16)

    hrows = [slice(h * hr, (h + 1) * hr) for h in range(heads)]
    ss = [_dot_nt(qs_ref[hrows[h], :], head_rows(k_refs, h)) for h in range(heads)]
    for h in range(heads):
        _softmax_step(ss[h], lambda pb: _dot(pb, head_rows(v_refs, h)), m_ref, l_ref, acc_ref,
                      hrows[h])

    @pl.when(j == pl.num_programs(1) - 1)
    def _():
        lam = _diff_lambda(lam_ref, lam_init)
        n = acc_ref[...] / l_ref[...]
        outs = []
        for h in range(heads):
            o = n[h * hr:h * hr + ts, :] - lam * n[h * hr + ts:(h + 1) * hr, :]
            outs.append(_subln(o, sg_ref[...], lam_init))
        o_ref[...] = jnp.concatenate(outs, axis=1)


def _sattn(page_table, q, kn, vn, lamv, sg, cache_k, cache_v, layer, lam_init, pp):
    bs, ts, w = q.shape
    n_pages = page_table.shape[1]
    prow = cache_k.shape[2] * cache_k.shape[3]
    ck = cache_k.reshape(cache_k.shape[0], cache_k.shape[1], prow, DV)
    cv = cache_v.reshape(cache_v.shape[0], cache_v.shape[1], prow, DV)
    rows = (w // DK) * ts

    def page_spec(p):
        return pl.BlockSpec((None, None, prow, DV),
                            lambda b, j, pt: (layer, pt[b, j * pp + p], 0, 0))

    new_spec = pl.BlockSpec((None, ts, w), lambda b, j, pt: (b, 0, 0))
    grid_spec = pltpu.PrefetchScalarGridSpec(
        num_scalar_prefetch=1,
        grid=(bs, n_pages // pp),
        in_specs=[new_spec, new_spec, new_spec, _const_spec(lamv.shape), _const_spec(sg.shape)]
        + [page_spec(p) for p in range(pp)] + [page_spec(p) for p in range(pp)],
        out_specs=new_spec,
        scratch_shapes=[
            pltpu.VMEM((rows, DV), BF16),
            pltpu.VMEM((rows, 1), F32), pltpu.VMEM((rows, 1), F32),
            pltpu.VMEM((rows, DV), F32),
        ],
    )
    return pl.pallas_call(
        functools.partial(_sattn_kernel, pp=pp, ts=ts, lam_init=lam_init),
        grid_spec=grid_spec,
        out_shape=jax.ShapeDtypeStruct((bs, ts, w), F32),
        compiler_params=_cparams(2),
        name="sattn",
    )(page_table, q, kn, vn, lamv, sg, *([ck] * pp), *([cv] * pp))


def _post_even_kernel(x_ref, a_ref, o_ref, g_ref, wo_ref, ng_ref, shf_ref, scf_ref, gf_ref,
                      wgu_ref, wd_ref, y_ref):
    d_a = a_ref.shape[1]
    m = _dot(a_ref[...], wo_ref[0:d_a, :]) + _dot(o_ref[...].astype(BF16), wo_ref[d_a:, :])
    x1 = x_ref[...] + g_ref[...] * m
    y_ref[...] = _ffn(x1, ng_ref[...], shf_ref[...], scf_ref[...], gf_ref[...], wgu_ref, wd_ref)


def _post_even(x, a, o, g, w_out, ng, shf, scf, gf, wgu, wd, tm):
    nb, t, d = x.shape
    per_row = g.shape[1] != 1
    mod = _rows_spec(tm, d, per_row)
    return pl.pallas_call(
        _post_even_kernel,
        grid=(nb, t // tm),
        in_specs=[
            _rows_spec(tm, d), _rows_spec(tm, a.shape[2]), _rows_spec(tm, o.shape[2]), mod,
            _const_spec(w_out.shape), _const_spec(ng.shape), mod, mod, mod,
            _layer_spec(wgu), _layer_spec(wd),
        ],
        out_specs=_rows_spec(tm, d),
        out_shape=jax.ShapeDtypeStruct((nb, t, d), F32),
        compiler_params=_cparams(2),
        name="post_even",
    )(x, a, o, g, w_out, ng, shf, scf, gf, wgu.stack, wd.stack)


def _odd_kernel(*refs, tm, short_rows, emit_v):
    (x_ref, sh_ref, sc_ref, ng0_ref, wi_ref, bi_ref, lg_ref, lb_ref, ws_ref, bimg_ref, wo_ref,
     g_ref, ng1_ref, shf_ref, scf_ref, gf_ref, wgu_ref, wd_ref) = refs[:18]
    if emit_v:
        y_ref, v_ref, mix_ref = refs[18:]
    else:
        y_ref, mix_ref = refs[18:]
    d_c = wo_ref.shape[0]
    gw = d_c // G_C

    x = x_ref[...]
    h = _rms_mod(x, ng0_ref[...], sh_ref[...], sc_ref[...]).astype(BF16)
    u = _gelu_tanh(_dot(h, wi_ref[:, 0:d_c]) + bi_ref[:, 0:d_c])
    v = _gelu_tanh(_dot(h, wi_ref[:, d_c:]) + bi_ref[:, d_c:])
    mu = jnp.mean(v, axis=-1, keepdims=True)
    vc = v - mu
    var = jnp.mean(vc * vc, axis=-1, keepdims=True)
    vn = vc * lax.rsqrt(var + EPS) * lg_ref[...] + lb_ref[...]
    if emit_v:
        v_ref[...] = vn
    vb = vn.astype(BF16)

    r = lax.broadcasted_iota(jnp.int32, (CHUNK, CHUNK), 0)
    c = lax.broadcasted_iota(jnp.int32, (CHUNK, CHUNK), 1)
    if short_rows is None:
        for gi in range(G_C):
            wsg = jnp.where(c <= r, ws_ref[gi], 0.0).astype(BF16)
            for n in range(tm // CHUNK):
                blk = _dot(wsg, vb[n * CHUNK:(n + 1) * CHUNK, gi * gw:(gi + 1) * gw])
                mix_ref[n * CHUNK:(n + 1) * CHUNK, gi * gw:(gi + 1) * gw] = (
                    blk + bimg_ref[:, gi * gw:(gi + 1) * gw])
    else:
        rr = lax.broadcasted_iota(jnp.int32, (tm, CHUNK), 0)
        cc = lax.broadcasted_iota(jnp.int32, (tm, CHUNK), 1)
        sel = jnp.where(jnp.bitwise_and(rr, short_rows - 1) == cc, 1.0, 0.0).astype(BF16)
        r2 = lax.broadcasted_iota(jnp.int32, (tm, tm), 0)
        c2 = lax.broadcasted_iota(jnp.int32, (tm, tm), 1)
        same_seq = _shr(r2, short_rows) == _shr(c2, short_rows)
        causal = jnp.bitwise_and(c2, short_rows - 1) <= jnp.bitwise_and(r2, short_rows - 1)
        keep = jnp.where(same_seq, jnp.where(causal, 1.0, 0.0), 0.0)
        for gi in range(G_C):
            spread = _dot_nt(_dot(sel, ws_ref[gi].astype(BF16)).astype(BF16), sel)
            mg = (spread * keep).astype(BF16)
            blk = _dot(mg, vb[:, gi * gw:(gi + 1) * gw])
            mix_ref[:, gi * gw:(gi + 1) * gw] = blk + bimg_ref[:, gi * gw:(gi + 1) * gw]

    y = _dot((u * mix_ref[...]).astype(BF16), wo_ref[...])
    x1 = x + g_ref[...] * y
    y_ref[...] = _ffn(x1, ng1_ref[...], shf_ref[...], scf_ref[...], gf_ref[...], wgu_ref, wd_ref)


def _odd(x, sh, sc, ng0, w_in, b_in, ln_g, ln_b, w_s, bimg, w_out, g, ng1, shf, scf, gf, wgu, wd,
         tm, short_rows, emit_v):
    nb, t, d = x.shape
    d_c = w_out.shape[0]
    per_row = sh.shape[1] != 1
    mod = _rows_spec(tm, d, per_row)
    out_specs = [_rows_spec(tm, d)]
    out_shape = [jax.ShapeDtypeStruct((nb, t, d), F32)]
    if emit_v:
        out_specs.append(_rows_spec(tm, d_c))
        out_shape.append(jax.ShapeDtypeStruct((nb, t, d_c), F32))
    return pl.pallas_call(
        functools.partial(_odd_kernel, tm=tm, short_rows=short_rows, emit_v=emit_v),
        grid=(nb, t // tm),
        in_specs=[
            _rows_spec(tm, d), mod, mod, _const_spec(ng0.shape), _const_spec(w_in.shape),
            _const_spec(b_in.shape), _const_spec(ln_g.shape), _const_spec(ln_b.shape),
            _const_spec(w_s.shape), _const_spec(bimg.shape), _const_spec(w_out.shape),
            mod, _const_spec(ng1.shape), mod, mod, mod,
            _layer_spec(wgu), _layer_spec(wd),
        ],
        out_specs=out_specs, out_shape=out_shape,
        scratch_shapes=[pltpu.VMEM((tm, d_c), F32)],
        compiler_params=_cparams(2),
        name="odd",
    )(x, sh, sc, ng0, w_in, b_in, ln_g, ln_b, w_s, bimg, w_out, g, ng1, shf, scf, gf, wgu.stack, wd.stack)


def kernel(x_prompt, x_sample, c_prompt, c_sample, cache_k, cache_v, state_conv, page_table, w_ada, b_ada, norm_g, w_in_even, conv_w, q_norm_g, k_norm_g, lam_q1, lam_k1, lam_q2, lam_k2, subln_g, w_out_even, w_in_odd, b_in_odd, ln_g, ln_b, w_s, b_s, w_out_odd, w_gate_up, w_down):
    bp, tp, d = x_prompt.shape
    bs, ts, _ = x_sample.shape
    depth = w_ada.shape[0]
    d_a = conv_w.shape[2]
    qk_w = H_B * 2 * DK
    rows_s = bs * ts

    mods = _adaln(jnp.concatenate([c_prompt, c_sample], axis=0), w_ada, b_ada)

    grp = jnp.arange(qk_w) // DK
    bd = jnp.where(grp[:, None] == grp[None, :], 1.0 / DK, 0.0).astype(BF16)

    wgu_all = w_gate_up.astype(BF16)
    wd_all = w_down.astype(BF16)
    xp = x_prompt
    xs = x_sample.reshape(1, rows_s, d)
    k_p, v_p, conv_p, k_s, v_s, conv_s, chunk_v = [], [], [], [], [], [], []

    for i in range(depth):
        mp = [m.reshape(bp, 1, d) for m in jnp.split(mods[i, :bp], 6, axis=-1)]
        ms = [jnp.repeat(m, ts, axis=0).reshape(1, rows_s, d)
              for m in jnp.split(mods[i, bp:], 6, axis=-1)]
        ng0 = norm_g[i, 0].reshape(1, d)
        ng1 = norm_g[i, 1].reshape(1, d)
        wgu = _LayerOf(wgu_all, i)
        wd = _LayerOf(wd_all, i)
        if i % 2 == 0:
            e = i // 2
            lam_init = 0.8 - 0.6 * math.exp(-0.3 * i)
            w_in = w_in_even[e].astype(BF16)
            w_out = w_out_even[e].astype(BF16)
            qg = jnp.tile(q_norm_g[e], qk_w // DK).reshape(1, qk_w)
            kg = jnp.tile(k_norm_g[e], qk_w // DK).reshape(1, qk_w)
            lamv = jnp.stack([lam_q1[e], lam_k1[e], lam_q2[e], lam_k2[e]])
            sg = subln_g[e].reshape(1, DV)

            a, q, k, v, kb, vt, tail = _inproj(xp, mp[0], mp[1], ng0, w_in, conv_w[e], qg, kg, bd,
                                               None, ROW_TILE)
            o = _attn(q, kb, vt, lamv, sg, lam_init, ATTN_TILE)
            xp = _post_even(xp, a, o, mp[2], w_out, ng1, mp[3], mp[4], mp[5], wgu, wd, ROW_TILE)
            k_p.append(k.reshape(bp, tp, H_B, 2 * DK))
            v_p.append(v.reshape(bp, tp, H_B, DV))
            conv_p.append(tail[:, 8 - (CONV_W - 1):, :])

            st = state_conv[e]
            zeros = jnp.zeros((bs, ts - 1, d_a), F32)
            fill1 = jnp.concatenate([st[:, 1:2], zeros], axis=1).reshape(1, rows_s, d_a)
            fill2 = jnp.concatenate([st, zeros[:, 1:]], axis=1).reshape(1, rows_s, d_a)
            a, q, k, v, _, _, xc = _inproj(xs, ms[0], ms[1], ng0, w_in, conv_w[e], qg, kg, bd,
                                           (fill1, fill2), rows_s)
            o = _sattn(page_table, q.astype(F32).reshape(bs, ts, qk_w), k.reshape(bs, ts, qk_w),
                       v.reshape(bs, ts, -1), lamv, sg, cache_k, cache_v, e, lam_init,
                       PAGES_PER_STEP)
            xs = _post_even(xs, a, o.reshape(1, rows_s, -1), ms[2], w_out, ng1, ms[3], ms[4], ms[5],
                            wgu, wd, rows_s)
            k_s.append(k.reshape(bs, ts, H_B, 2 * DK))
            v_s.append(v.reshape(bs, ts, H_B, DV))
            conv_s.append(xc.reshape(bs, ts, d_a)[:, ts - (CONV_W - 1):, :])
        else:
            oi = i // 2
            d_c = w_out_odd.shape[1]
            w_in = w_in_odd[oi].astype(BF16)
            w_out = w_out_odd[oi].astype(BF16)
            b_in = b_in_odd[oi].reshape(1, -1)
            lg = ln_g[oi].reshape(1, d_c)
            lb = ln_b[oi].reshape(1, d_c)
            bimg = jnp.repeat(b_s[oi].T, d_c // G_C, axis=1)
            xp = _odd(xp, mp[0], mp[1], ng0, w_in, b_in, lg, lb, w_s[oi], bimg, w_out, mp[2], ng1,
                      mp[3], mp[4], mp[5], wgu, wd, ROW_TILE, None, False)[0]
            bimg_s = jnp.tile(bimg[:ts], (bs, 1))
            xs, vrows = _odd(xs, ms[0], ms[1], ng0, w_in, b_in, lg, lb, w_s[oi], bimg_s, w_out, ms[2],
                             ng1, ms[3], ms[4], ms[5], wgu, wd, rows_s, ts, True)
            chunk_v.append(vrows.reshape(bs, ts, d_c))

    return (xp, xs.reshape(bs, ts, d), jnp.stack(k_p), jnp.stack(v_p), jnp.stack(conv_p),
            jnp.stack(k_s), jnp.stack(v_s), jnp.stack(conv_s), jnp.stack(chunk_v))
```

```python
import functools
import math
from typing import NamedTuple

import jax
import jax.numpy as jnp
from jax import lax
from jax.experimental import pallas as pl
from jax.experimental.pallas import tpu as pltpu

F32 = jnp.float32
BF16 = jnp.bfloat16
EPS = 1e-6

A_GROUPS = 8
CONV_W = 3
H_B = 4
DK = 64
DV = 2 * DK
CHUNK = 128
G_C = 8
Q_SCALE = DK ** -0.5 * math.log2(math.e)

VMEM_LIMIT_BYTES = 56 * 1024 * 1024
MXU_TILE = 256
BF16_ROWS = 16

ROW_TILE = 512
ATTN_TILE = 512
ATTN_COLS = 512
ATTN_HEADS = 4
PAGES_PER_STEP = 32


def _cparams(n_axes):
    return pltpu.CompilerParams(
        dimension_semantics=("arbitrary",) * n_axes,
        vmem_limit_bytes=VMEM_LIMIT_BYTES)


def _const_spec(shape):
    zeros = (0,) * len(shape)
    return pl.BlockSpec(shape, lambda *_: zeros, pipeline_mode=pl.Buffered(1))


class _LayerOf(NamedTuple):
    stack: jax.Array
    layer: int


def _layer_spec(w):
    shape = w.stack.shape
    index = (w.layer,) + (0,) * (len(shape) - 1)
    return pl.BlockSpec((None,) + shape[1:], lambda *_: index, pipeline_mode=pl.Buffered(1))


def _rows_spec(tm, width, per_row=True):
    if per_row:
        return pl.BlockSpec((None, tm, width), lambda b, i: (b, i, 0))
    return pl.BlockSpec((None, 1, width), lambda b, i: (b, 0, 0))


def _dot(a, b):
    return jnp.dot(a, b, preferred_element_type=F32)


def _dot_nt(a, b):
    return lax.dot_general(a, b, (((1,), (1,)), ((), ())), preferred_element_type=F32)


def _rms_mod(x, g, shift, scale):
    ms = jnp.mean(x * x, axis=-1, keepdims=True)
    y = x * lax.rsqrt(ms + EPS) * g
    return y * (1.0 + scale) + shift


def _shr(x, pow2):
    shift = pow2.bit_length() - 1
    assert 1 << shift == pow2
    return jnp.right_shift(x, shift)


def _silu(x):
    return x / (1.0 + jnp.exp(-x))


def _gelu_tanh(x):
    c = math.sqrt(2.0 / math.pi)
    half = 0.5 * x
    return half + half * jnp.tanh(x * (c + (c * 0.044715) * (x * x)))


def _ffn(x1, ng, shf, scf, gf, wgu_ref, wd_ref):
    d_ff = wd_ref.shape[0]
    n_tiles, rem = divmod(d_ff, MXU_TILE)
    assert rem == 0
    split = (n_tiles + 1) // 2 * MXU_TILE
    h = _rms_mod(x1, ng, shf, scf).astype(BF16)
    acc = None
    for lo, hi in ((0, split), (split, d_ff)):
        g = _dot(h, wgu_ref[:, lo:hi])
        u = _dot(h, wgu_ref[:, d_ff + lo:d_ff + hi])
        a = (_silu(g) * u).astype(BF16)
        d = _dot(a, wd_ref[lo:hi, :])
        acc = d if acc is None else acc + d
    return x1 + gf * acc


def _adaln_kernel(c_ref, w_ref, b_ref, o_ref):
    s = _silu(c_ref[...])
    o_ref[...] = jnp.dot(s, w_ref[...], precision=lax.Precision.HIGHEST,
                         preferred_element_type=F32) + b_ref[...]


def _adaln(c_all, w_ada, b_ada):
    depth, d, d6 = w_ada.shape
    r = c_all.shape[0]
    tn = d6 // 4
    return pl.pallas_call(
        _adaln_kernel,
        grid=(depth, d6 // tn),
        in_specs=[
            pl.BlockSpec((r, d), lambda l, j: (0, 0)),
            pl.BlockSpec((None, d, tn), lambda l, j: (l, 0, j)),
            pl.BlockSpec((None, 1, tn), lambda l, j: (l, 0, j)),
        ],
        out_specs=pl.BlockSpec((None, r, tn), lambda l, j: (l, 0, j)),
        out_shape=jax.ShapeDtypeStruct((depth, r, d6), F32),
        compiler_params=_cparams(2),
        name="adaln",
    )(c_all, w_ada, b_ada.reshape(depth, 1, d6))


def _inproj_kernel(*refs, tm, d_a, qk_w, has_fill, period):
    if has_fill:
        (x_ref, sh_ref, sc_ref, ng_ref, w_ref, cw_ref, qg_ref, kg_ref, bd_ref, f1_ref, f2_ref,
         a_ref, q_ref, k_ref, v_ref, kb_ref, vt_ref, tail_ref) = refs
    else:
        (x_ref, sh_ref, sc_ref, ng_ref, w_ref, cw_ref, qg_ref, kg_ref, bd_ref,
         a_ref, q_ref, k_ref, v_ref, kb_ref, vt_ref, tail_ref, carry_ref) = refs

    h = _rms_mod(x_ref[...], ng_ref[...], sh_ref[...], sc_ref[...]).astype(BF16)

    a_b = _dot(h, w_ref[:, 0:d_a])
    xc = _dot(h, w_ref[:, d_a:2 * d_a]) * _dot(h, w_ref[:, 2 * d_a:3 * d_a])
    row = lax.broadcasted_iota(jnp.int32, (tm, d_a), 0)
    if has_fill:
        rp = jnp.bitwise_and(row, period - 1)
        fill1 = f1_ref[...]
        fill2 = f2_ref[...]
    else:
        rp = row

        @pl.when(pl.program_id(1) == 0)
        def _():
            carry_ref[...] = jnp.zeros_like(carry_ref)

        last1 = carry_ref[pl.ds(7, 1), :]
        last2 = carry_ref[pl.ds(6, 1), :]
        fill1 = jnp.broadcast_to(last1, (tm, d_a))
        fill2 = jnp.where(row == 0, last2, last1)
    prev1 = jnp.where(rp < 1, fill1, pltpu.roll(xc, 1, 0))
    prev2 = jnp.where(rp < 2, fill2, pltpu.roll(xc, 2, 0))
    y = cw_ref[0:1, :] * prev2 + cw_ref[1:2, :] * prev1 + cw_ref[2:3, :] * xc
    a_ref[...] = (a_b * y).astype(BF16)
    tail_rows = tail_ref.shape[0]
    tail_ref[...] = xc[tm - tail_rows:tm, :]
    if not has_fill:
        carry_ref[...] = xc[tm - 8:tm, :]

    q0 = 3 * d_a
    zq = _dot(h, w_ref[:, q0:q0 + qk_w])
    msq = _dot((zq * zq).astype(BF16), bd_ref[...])
    qn = zq * lax.rsqrt(msq + EPS) * qg_ref[...]
    q_ref[...] = (qn * Q_SCALE).astype(BF16)
    zk = _dot(h, w_ref[:, q0 + qk_w:q0 + 2 * qk_w])
    msk = _dot((zk * zk).astype(BF16), bd_ref[...])
    kn = zk * lax.rsqrt(msk + EPS) * kg_ref[...]
    kb_ref[...] = kn.astype(BF16)
    zv = _dot(h, w_ref[:, q0 + 2 * qk_w:])
    for out_ref, val in ((k_ref, kn), (v_ref, zv)):
        if has_fill:
            out_ref[...] = val
        else:
            heads = val.shape[1] // DV
            for hh in range(heads):
                out_ref[pl.ds(hh, tm, stride=heads), :] = val[:, hh * DV:(hh + 1) * DV]
    vt_ref[...] = zv.T.astype(BF16)


def _inproj(x, sh, sc, ng, w_in, cw, qg, kg, bd, fills, tm):
    nb, t, d = x.shape
    d_a = cw.shape[1]
    qk_w = qg.shape[1]
    d_b = w_in.shape[1] - 3 * d_a - 2 * qk_w
    has_fill = fills is not None
    per_row = sh.shape[1] != 1
    tail_rows = tm if has_fill else 8
    in_specs = [
        _rows_spec(tm, d), _rows_spec(tm, d, per_row), _rows_spec(tm, d, per_row),
        _const_spec(ng.shape), _const_spec(w_in.shape), _const_spec(cw.shape),
        _const_spec(qg.shape), _const_spec(kg.shape), _const_spec(bd.shape),
    ]
    args = [x, sh, sc, ng, w_in, cw, qg, kg, bd]
    scratch = []
    if has_fill:
        in_specs += [_rows_spec(tm, d_a), _rows_spec(tm, d_a)]
        args += list(fills)
        tail_spec = _rows_spec(tm, d_a)
        tail_shape = (nb, t, d_a)
    else:
        scratch = [pltpu.VMEM((8, d_a), F32)]
        tail_spec = pl.BlockSpec((None, 8, d_a), lambda b, i: (b, 0, 0))
        tail_shape = (nb, 8, d_a)
    if has_fill:
        kv_specs = [_rows_spec(tm, qk_w), _rows_spec(tm, d_b)]
        kv_shapes = [(nb, t, qk_w), (nb, t, d_b)]
    else:
        kv_specs = [_rows_spec(tm * (qk_w // DV), DV), _rows_spec(tm * (d_b // DV), DV)]
        kv_shapes = [(nb, t * (qk_w // DV), DV), (nb, t * (d_b // DV), DV)]
    out_specs = [_rows_spec(tm, d_a), _rows_spec(tm, qk_w), *kv_specs,
                 _rows_spec(tm, qk_w), pl.BlockSpec((None, d_b, tm), lambda b, i: (b, 0, i)), tail_spec]
    out_shape = [
        jax.ShapeDtypeStruct((nb, t, d_a), BF16), jax.ShapeDtypeStruct((nb, t, qk_w), BF16),
        jax.ShapeDtypeStruct(kv_shapes[0], F32), jax.ShapeDtypeStruct(kv_shapes[1], F32),
        jax.ShapeDtypeStruct((nb, t, qk_w), BF16), jax.ShapeDtypeStruct((nb, d_b, t), BF16),
        jax.ShapeDtypeStruct(tail_shape, F32),
    ]
    return pl.pallas_call(
        functools.partial(_inproj_kernel, tm=tm, d_a=d_a, qk_w=qk_w, has_fill=has_fill, period=8),
        grid=(nb, t // tm),
        in_specs=in_specs, out_specs=out_specs, out_shape=out_shape,
        scratch_shapes=scratch,
        compiler_params=_cparams(2),
        name="inproj",
    )(*args)


def _diff_lambda(lam_ref, lam_init):
    lv = lam_ref[...]
    e1 = jnp.exp(jnp.sum(lv[0:1, :] * lv[1:2, :], axis=-1, keepdims=True))
    e2 = jnp.exp(jnp.sum(lv[2:3, :] * lv[3:4, :], axis=-1, keepdims=True))
    return e1 - e2 + lam_init


def _subln(o, g, lam_init):
    ms = jnp.mean(o * o, axis=-1, keepdims=True)
    return o * lax.rsqrt(ms + EPS) * g * (1.0 - lam_init)


def _softmax_step(s, v_dot, m_ref, l_ref, acc_ref, rows=slice(None)):
    m_prev = m_ref[rows, :]
    m_new = jnp.maximum(m_prev, jnp.max(s, axis=-1, keepdims=True))
    alpha = jnp.exp2(m_prev - m_new)
    p = jnp.exp2(s - m_new)
    l_ref[rows, :] = alpha * l_ref[rows, :] + jnp.sum(p, axis=-1, keepdims=True)
    acc_ref[rows, :] = alpha * acc_ref[rows, :] + v_dot(p.astype(BF16))
    m_ref[rows, :] = m_new


def _attn_kernel(q_ref, k_ref, vt_ref, lam_ref, sg_ref, o_ref, qs_ref, sa_ref, sb_ref, m_ref, acc_ref,
                 *, tq, hps, lam_init):
    i = pl.program_id(2)
    lane = lax.broadcasted_iota(jnp.int32, (tq, DV), 1)
    for hh in range(hps):
        q = q_ref[:, hh * DV:(hh + 1) * DV]
        zero = jnp.zeros_like(q)
        qs_ref[hh, 0:tq, :] = jnp.where(lane < DK, q, zero)
        qs_ref[hh, tq:2 * tq, :] = jnp.where(lane >= DK, q, zero)
    m_ref[...] = jnp.full_like(m_ref, -jnp.inf)
    acc_ref[...] = jnp.zeros_like(acc_ref)

    chains = [(hh, c0, hh * 2 * tq + c0) for hh in range(hps) for c0 in range(0, 2 * tq, ATTN_COLS)]
    ones = jnp.ones((acc_ref.shape[0] - DV, tq), BF16)

    def scores(j, s_ref):
        keys = pl.ds(pl.multiple_of(j * tq, tq), tq)
        for hh, c0, g0 in chains:
            k = k_ref[keys, hh * DV:(hh + 1) * DV]
            s_ref[:, g0:g0 + ATTN_COLS] = _dot_nt(k, qs_ref[hh, c0:c0 + ATTN_COLS, :])

    def step(j, s_ref, diagonal):
        keys = pl.ds(pl.multiple_of(j * tq, tq), tq)
        vts = [jnp.concatenate([vt_ref[hh * DV:(hh + 1) * DV, keys], ones], axis=0)
               for hh in range(hps)]
        for hh, c0, g0 in chains:
            cols = slice(g0, g0 + ATTN_COLS)
            st = s_ref[:, cols]
            if diagonal:
                key = lax.broadcasted_iota(jnp.int32, st.shape, 0)
                qrow = lax.broadcasted_iota(jnp.int32, st.shape, 1) + (c0 % tq)
                st = jnp.where(key <= qrow, st, -jnp.inf)
            m_prev = m_ref[:, cols]
            m_new = jnp.maximum(m_prev, jnp.max(st, axis=0, keepdims=True))
            alpha = jnp.exp2(m_prev - m_new)
            pt = jnp.exp2(st - m_new)
            acc_ref[:, cols] = alpha * acc_ref[:, cols] + _dot(vts[hh], pt.astype(BF16))
            m_ref[:, cols] = m_new

    scores(0, sa_ref)

    def pair(jj, carry):
        j0 = 2 * jj
        scores(j0 + 1, sb_ref)
        step(j0, sa_ref, False)
        scores(j0 + 2, sa_ref)
        step(j0 + 1, sb_ref, False)
        return carry

    lax.fori_loop(0, i // 2, pair, 0)

    @pl.when(i % 2 == 0)
    def _():
        step(i, sa_ref, True)

    @pl.when(i % 2 == 1)
    def _():
        scores(i, sb_ref)
        step(i - 1, sa_ref, False)
        step(i, sb_ref, True)

    lam = _diff_lambda(lam_ref, lam_init)
    for hh in range(hps):
        h0 = hh * 2 * tq
        nt = acc_ref[0:DV, h0:h0 + 2 * tq] / acc_ref[DV:DV + 1, h0:h0 + 2 * tq]
        ot = nt[:, 0:tq] - lam * nt[:, tq:2 * tq]
        ms = jnp.mean(ot * ot, axis=0, keepdims=True)
        ot = ot * lax.rsqrt(ms + EPS)
        o_ref[:, hh * DV:(hh + 1) * DV] = (ot.T * sg_ref[...] * (1.0 - lam_init)).astype(o_ref.dtype)


def _attn(q, kb, vt, lamv, sg, lam_init, tq, hps):
    b, t, w = q.shape
    hw = hps * DV
    return pl.pallas_call(
        functools.partial(_attn_kernel, tq=tq, hps=hps, lam_init=lam_init),
        grid=(b, w // hw, t // tq),
        in_specs=[
            pl.BlockSpec((None, tq, hw), lambda b_, h, i: (b_, i, h)),
            pl.BlockSpec((None, t, hw), lambda b_, h, i: (b_, 0, h)),
            pl.BlockSpec((None, hw, t), lambda b_, h, i: (b_, h, 0)),
            _const_spec(lamv.shape), _const_spec(sg.shape),
        ],
        out_specs=pl.BlockSpec((None, tq, hw), lambda b_, h, i: (b_, i, h)),
        out_shape=jax.ShapeDtypeStruct((b, t, w), BF16),
        scratch_shapes=[
            pltpu.VMEM((hps, 2 * tq, DV), BF16),
            pltpu.VMEM((tq, hps * 2 * tq), F32), pltpu.VMEM((tq, hps * 2 * tq), F32),
            pltpu.VMEM((1, hps * 2 * tq), F32),
            pltpu.VMEM((DV + BF16_ROWS, hps * 2 * tq), F32),
        ],
        compiler_params=_cparams(3),
        name="attn",
    )(q, kb, vt, lamv, sg)


def _sattn_kernel(pt_ref, q_ref, kn_ref, vn_ref, lam_ref, sg_ref, *rest, pp, ts, lam_init):
    k_refs = rest[0:pp]
    v_refs = rest[pp:2 * pp]
    o_ref, qs_ref, m_ref, l_ref, acc_ref = rest[2 * pp:]
    j = pl.program_id(1)
    rows = qs_ref.shape[0]
    hr = 2 * ts
    heads = rows // hr
    page = k_refs[0].shape[0] // heads

    @pl.when(j == 0)
    def _():
        q = q_ref[...]
        lane = lax.broadcasted_iota(jnp.int32, (ts, DV), 1)
        parts = []
        for h in range(heads):
            qh = q[:, h * DV:(h + 1) * DV]
            parts += [jnp.where(lane < DK, qh, 0.0), jnp.where(lane >= DK, qh, 0.0)]
        qs = jnp.concatenate(parts, axis=0).astype(BF16)
        qs_ref[...] = qs
        m_ref[...] = jnp.full_like(m_ref, -jnp.inf)
        l_ref[...] = jnp.zeros_like(l_ref)
        acc_ref[...] = jnp.zeros_like(acc_ref)
        pad = jnp.zeros((DV - ts, DV), F32)
        kns = [jnp.concatenate([kn_ref[:, h * DV:(h + 1) * DV], pad], axis=0).astype(BF16)
               for h in range(heads)]
        vns = [jnp.concatenate([vn_ref[:, h * DV:(h + 1) * DV], pad], axis=0).astype(BF16)
               for h in range(heads)]
        s = jnp.concatenate([_dot_nt(qs[h * hr:(h + 1) * hr, :], kns[h]) for h in range(heads)],
                            axis=0)
        rr = lax.broadcasted_iota(jnp.int32, s.shape, 0)
        cc = lax.broadcasted_iota(jnp.int32, s.shape, 1)
        s = jnp.where(cc <= jnp.bitwise_and(rr, ts - 1), s, -jnp.inf)

        def new_v_dot(pb):
            return jnp.concatenate([_dot(pb[h * hr:(h + 1) * hr, :], vns[h]) for h in range(heads)],
                                   axis=0)

        _softmax_step(s, new_v_dot, m_ref, l_ref, acc_ref)

    def head_rows(refs, h):
        return jnp.concatenate([r[pl.ds(h, page, stride=heads), :] for r in refs],
                               axis=0).astype(BF16)

    hrows = [slice(h * hr, (h + 1) * hr) for h in range(heads)]
    ss = [_dot_nt(qs_ref[hrows[h], :], head_rows(k_refs, h)) for h in range(heads)]
    for h in range(heads):
        _softmax_step(ss[h], lambda pb: _dot(pb, head_rows(v_refs, h)), m_ref, l_ref, acc_ref,
                      hrows[h])

    @pl.when(j == pl.num_programs(1) - 1)
    def _():
        lam = _diff_lambda(lam_ref, lam_init)
        n = acc_ref[...] / l_ref[...]
        outs = []
        for h in range(heads):
            o = n[h * hr:h * hr + ts, :] - lam * n[h * hr + ts:(h + 1) * hr, :]
            outs.append(_subln(o, sg_ref[...], lam_init))
        o_ref[...] = jnp.concatenate(outs, axis=1)


def _sattn(page_table, q, kn, vn, lamv, sg, cache_k, cache_v, layer, lam_init, pp):
    bs, ts, w = q.shape
    n_pages = page_table.shape[1]
    prow = cache_k.shape[2] * cache_k.shape[3]
    ck = cache_k.reshape(cache_k.shape[0], cache_k.shape[1], prow, DV)
    cv = cache_v.reshape(cache_v.shape[0], cache_v.shape[1], prow, DV)
    rows = (w // DK) * ts

    def page_spec(p):
        return pl.BlockSpec((None, None, prow, DV),
                            lambda b, j, pt: (layer, pt[b, j * pp + p], 0, 0))

    new_spec = pl.BlockSpec((None, ts, w), lambda b, j, pt: (b, 0, 0))
    grid_spec = pltpu.PrefetchScalarGridSpec(
        num_scalar_prefetch=1,
        grid=(bs, n_pages // pp),
        in_specs=[new_spec, new_spec, new_spec, _const_spec(lamv.shape), _const_spec(sg.shape)]
        + [page_spec(p) for p in range(pp)] + [page_spec(p) for p in range(pp)],
        out_specs=new_spec,
        scratch_shapes=[
            pltpu.VMEM((rows, DV), BF16),
            pltpu.VMEM((rows, 1), F32), pltpu.VMEM((rows, 1), F32),
            pltpu.VMEM((rows, DV), F32),
        ],
    )
    return pl.pallas_call(
        functools.partial(_sattn_kernel, pp=pp, ts=ts, lam_init=lam_init),
        grid_spec=grid_spec,
        out_shape=jax.ShapeDtypeStruct((bs, ts, w), F32),
        compiler_params=_cparams(2),
        name="sattn",
    )(page_table, q, kn, vn, lamv, sg, *([ck] * pp), *([cv] * pp))


def _post_even_kernel(x_ref, a_ref, o_ref, g_ref, wo_ref, ng_ref, shf_ref, scf_ref, gf_ref,
                      wgu_ref, wd_ref, y_ref):
    d_a = a_ref.shape[1]
    m = _dot(a_ref[...], wo_ref[0:d_a, :]) + _dot(o_ref[...].astype(BF16), wo_ref[d_a:, :])
    x1 = x_ref[...] + g_ref[...] * m
    y_ref[...] = _ffn(x1, ng_ref[...], shf_ref[...], scf_ref[...], gf_ref[...], wgu_ref, wd_ref)


def _post_even(x, a, o, g, w_out, ng, shf, scf, gf, wgu, wd, tm):
    nb, t, d = x.shape
    per_row = g.shape[1] != 1
    mod = _rows_spec(tm, d, per_row)
    return pl.pallas_call(
        _post_even_kernel,
        grid=(nb, t // tm),
        in_specs=[
            _rows_spec(tm, d), _rows_spec(tm, a.shape[2]), _rows_spec(tm, o.shape[2]), mod,
            _const_spec(w_out.shape), _const_spec(ng.shape), mod, mod, mod,
            _layer_spec(wgu), _layer_spec(wd),
        ],
        out_specs=_rows_spec(tm, d),
        out_shape=jax.ShapeDtypeStruct((nb, t, d), F32),
        compiler_params=_cparams(2),
        name="post_even",
    )(x, a, o, g, w_out, ng, shf, scf, gf, wgu.stack, wd.stack)


def _odd_kernel(*refs, tm, short_rows, emit_v):
    (x_ref, sh_ref, sc_ref, ng0_ref, wi_ref, bi_ref, lg_ref, lb_ref, ws_ref, bimg_ref, wo_ref,
     g_ref, ng1_ref, shf_ref, scf_ref, gf_ref, wgu_ref, wd_ref) = refs[:18]
    if emit_v:
        y_ref, v_ref, mix_ref = refs[18:]
    else:
        y_ref, mix_ref = refs[18:]
    d_c = wo_ref.shape[0]
    gw = d_c // G_C

    x = x_ref[...]
    h = _rms_mod(x, ng0_ref[...], sh_ref[...], sc_ref[...]).astype(BF16)
    u = _gelu_tanh(_dot(h, wi_ref[:, 0:d_c]) + bi_ref[:, 0:d_c])
    v = _gelu_tanh(_dot(h, wi_ref[:, d_c:]) + bi_ref[:, d_c:])
    mu = jnp.mean(v, axis=-1, keepdims=True)
    vc = v - mu
    var = jnp.mean(vc * vc, axis=-1, keepdims=True)
    vn = vc * lax.rsqrt(var + EPS) * lg_ref[...] + lb_ref[...]
    if emit_v:
        v_ref[...] = vn
    vb = vn.astype(BF16)

    r = lax.broadcasted_iota(jnp.int32, (CHUNK, CHUNK), 0)
    c = lax.broadcasted_iota(jnp.int32, (CHUNK, CHUNK), 1)
    if short_rows is None:
        for gi in range(G_C):
            wsg = jnp.where(c <= r, ws_ref[gi], 0.0).astype(BF16)
            for n in range(tm // CHUNK):
                blk = _dot(wsg, vb[n * CHUNK:(n + 1) * CHUNK, gi * gw:(gi + 1) * gw])
                mix_ref[n * CHUNK:(n + 1) * CHUNK, gi * gw:(gi + 1) * gw] = (
                    blk + bimg_ref[:, gi * gw:(gi + 1) * gw])
    else:
        rr = lax.broadcasted_iota(jnp.int32, (tm, CHUNK), 0)
        cc = lax.broadcasted_iota(jnp.int32, (tm, CHUNK), 1)
        sel = jnp.where(jnp.bitwise_and(rr, short_rows - 1) == cc, 1.0, 0.0).astype(BF16)
        r2 = lax.broadcasted_iota(jnp.int32, (tm, tm), 0)
        c2 = lax.broadcasted_iota(jnp.int32, (tm, tm), 1)
        same_seq = _shr(r2, short_rows) == _shr(c2, short_rows)
        causal = jnp.bitwise_and(c2, short_rows - 1) <= jnp.bitwise_and(r2, short_rows - 1)
        keep = jnp.where(same_seq, jnp.where(causal, 1.0, 0.0), 0.0)
        for gi in range(G_C):
            spread = _dot_nt(_dot(sel, ws_ref[gi].astype(BF16)).astype(BF16), sel)
            mg = (spread * keep).astype(BF16)
            blk = _dot(mg, vb[:, gi * gw:(gi + 1) * gw])
            mix_ref[:, gi * gw:(gi + 1) * gw] = blk + bimg_ref[:, gi * gw:(gi + 1) * gw]

    y = _dot((u * mix_ref[...]).astype(BF16), wo_ref[...])
    x1 = x + g_ref[...] * y
    y_ref[...] = _ffn(x1, ng1_ref[...], shf_ref[...], scf_ref[...], gf_ref[...], wgu_ref, wd_ref)


def _odd(x, sh, sc, ng0, w_in, b_in, ln_g, ln_b, w_s, bimg, w_out, g, ng1, shf, scf, gf, wgu, wd,
         tm, short_rows, emit_v):
    nb, t, d = x.shape
    d_c = w_out.shape[0]
    per_row = sh.shape[1] != 1
    mod = _rows_spec(tm, d, per_row)
    out_specs = [_rows_spec(tm, d)]
    out_shape = [jax.ShapeDtypeStruct((nb, t, d), F32)]
    if emit_v:
        out_specs.append(_rows_spec(tm, d_c))
        out_shape.append(jax.ShapeDtypeStruct((nb, t, d_c), F32))
    return pl.pallas_call(
        functools.partial(_odd_kernel, tm=tm, short_rows=short_rows, emit_v=emit_v),
        grid=(nb, t // tm),
        in_specs=[
            _rows_spec(tm, d), mod, mod, _const_spec(ng0.shape), _const_spec(w_in.shape),
            _const_spec(b_in.shape), _const_spec(ln_g.shape), _const_spec(ln_b.shape),
            _const_spec(w_s.shape), _const_spec(bimg.shape), _const_spec(w_out.shape),
            mod, _const_spec(ng1.shape), mod, mod, mod,
            _layer_spec(wgu), _layer_spec(wd),
        ],
        out_specs=out_specs, out_shape=out_shape,
        scratch_shapes=[pltpu.VMEM((tm, d_c), F32)],
        compiler_params=_cparams(2),
        name="odd",
    )(x, sh, sc, ng0, w_in, b_in, ln_g, ln_b, w_s, bimg, w_out, g, ng1, shf, scf, gf, wgu.stack, wd.stack)


def kernel(x_prompt, x_sample, c_prompt, c_sample, cache_k, cache_v, state_conv, page_table, w_ada, b_ada, norm_g, w_in_even, conv_w, q_norm_g, k_norm_g, lam_q1, lam_k1, lam_q2, lam_k2, subln_g, w_out_even, w_in_odd, b_in_odd, ln_g, ln_b, w_s, b_s, w_out_odd, w_gate_up, w_down):
    bp, tp, d = x_prompt.shape
    bs, ts, _ = x_sample.shape
    depth = w_ada.shape[0]
    d_a = conv_w.shape[2]
    qk_w = H_B * 2 * DK
    rows_s = bs * ts

    mods = _adaln(jnp.concatenate([c_prompt, c_sample], axis=0), w_ada, b_ada)

    grp = jnp.arange(qk_w) // DK
    bd = jnp.where(grp[:, None] == grp[None, :], 1.0 / DK, 0.0).astype(BF16)

    wgu_all = w_gate_up.astype(BF16)
    wd_all = w_down.astype(BF16)
    xp = x_prompt
    xs = x_sample.reshape(1, rows_s, d)
    k_p, v_p, conv_p, k_s, v_s, conv_s, chunk_v = [], [], [], [], [], [], []

    for i in range(depth):
        mp = [m.reshape(bp, 1, d) for m in jnp.split(mods[i, :bp], 6, axis=-1)]
        ms = [jnp.repeat(m, ts, axis=0).reshape(1, rows_s, d)
              for m in jnp.split(mods[i, bp:], 6, axis=-1)]
        ng0 = norm_g[i, 0].reshape(1, d)
        ng1 = norm_g[i, 1].reshape(1, d)
        wgu = _LayerOf(wgu_all, i)
        wd = _LayerOf(wd_all, i)
        if i % 2 == 0:
            e = i // 2
            lam_init = 0.8 - 0.6 * math.exp(-0.3 * i)
            w_in = w_in_even[e].astype(BF16)
            w_out = w_out_even[e].astype(BF16)
            qg = jnp.tile(q_norm_g[e], qk_w // DK).reshape(1, qk_w)
            kg = jnp.tile(k_norm_g[e], qk_w // DK).reshape(1, qk_w)
            lamv = jnp.stack([lam_q1[e], lam_k1[e], lam_q2[e], lam_k2[e]])
            sg = subln_g[e].reshape(1, DV)

            a, q, k, v, kb, vt, tail = _inproj(xp, mp[0], mp[1], ng0, w_in, conv_w[e], qg, kg, bd,
                                               None, ROW_TILE)
            o = _attn(q, kb, vt, lamv, sg, lam_init, ATTN_TILE, ATTN_HEADS)
            xp = _post_even(xp, a, o, mp[2], w_out, ng1, mp[3], mp[4], mp[5], wgu, wd, ROW_TILE)
            k_p.append(k.reshape(bp, tp, H_B, 2 * DK))
            v_p.append(v.reshape(bp, tp, H_B, DV))
            conv_p.append(tail[:, 8 - (CONV_W - 1):, :])

            st = state_conv[e]
            zeros = jnp.zeros((bs, ts - 1, d_a), F32)
            fill1 = jnp.concatenate([st[:, 1:2], zeros], axis=1).reshape(1, rows_s, d_a)
            fill2 = jnp.concatenate([st, zeros[:, 1:]], axis=1).reshape(1, rows_s, d_a)
            a, q, k, v, _, _, xc = _inproj(xs, ms[0], ms[1], ng0, w_in, conv_w[e], qg, kg, bd,
                                           (fill1, fill2), rows_s)
            o = _sattn(page_table, q.astype(F32).reshape(bs, ts, qk_w), k.reshape(bs, ts, qk_w),
                       v.reshape(bs, ts, -1), lamv, sg, cache_k, cache_v, e, lam_init,
                       PAGES_PER_STEP)
            xs = _post_even(xs, a, o.reshape(1, rows_s, -1), ms[2], w_out, ng1, ms[3], ms[4], ms[5],
                            wgu, wd, rows_s)
            k_s.append(k.reshape(bs, ts, H_B, 2 * DK))
            v_s.append(v.reshape(bs, ts, H_B, DV))
            conv_s.append(xc.reshape(bs, ts, d_a)[:, ts - (CONV_W - 1):, :])
        else:
            oi = i // 2
            d_c = w_out_odd.shape[1]
            w_in = w_in_odd[oi].astype(BF16)
            w_out = w_out_odd[oi].astype(BF16)
            b_in = b_in_odd[oi].reshape(1, -1)
            lg = ln_g[oi].reshape(1, d_c)
            lb = ln_b[oi].reshape(1, d_c)
            bimg = jnp.repeat(b_s[oi].T, d_c // G_C, axis=1)
            xp = _odd(xp, mp[0], mp[1], ng0, w_in, b_in, lg, lb, w_s[oi], bimg, w_out, mp[2], ng1,
                      mp[3], mp[4], mp[5], wgu, wd, ROW_TILE, None, False)[0]
            bimg_s = jnp.tile(bimg[:ts], (bs, 1))
            xs, vrows = _odd(xs, ms[0], ms[1], ng0, w_in, b_in, lg, lb, w_s[oi], bimg_s, w_out, ms[2],
                             ng1, ms[3], ms[4], ms[5], wgu, wd, rows_s, ts, True)
            chunk_v.append(vrows.reshape(bs, ts, d_c))

    return (xp, xs.reshape(bs, ts, d), jnp.stack(k_p), jnp.stack(v_p), jnp.stack(conv_p),
            jnp.stack(k_s), jnp.stack(v_s), jnp.stack(conv_s), jnp.stack(chunk_v))
```

```python
import functools
import math
from typing import NamedTuple

import jax
import jax.numpy as jnp
from jax import lax
from jax.experimental import pallas as pl
from jax.experimental.pallas import tpu as pltpu

F32 = jnp.float32
BF16 = jnp.bfloat16
EPS = 1e-6

A_GROUPS = 8
CONV_W = 3
H_B = 4
DK = 64
DV = 2 * DK
CHUNK = 128
G_C = 8
Q_SCALE = DK ** -0.5 * math.log2(math.e)

VMEM_LIMIT_BYTES = 56 * 1024 * 1024
MXU_TILE = 256
BF16_ROWS = 16

ROW_TILE = 512
INPROJ_SUBTILES = 2
ATTN_TILE = 512
ATTN_COLS = 512
ATTN_HEADS = 4
PAGES_PER_STEP = 32


def _cparams(n_axes):
    return pltpu.CompilerParams(
        dimension_semantics=("arbitrary",) * n_axes,
        vmem_limit_bytes=VMEM_LIMIT_BYTES)


def _const_spec(shape):
    zeros = (0,) * len(shape)
    return pl.BlockSpec(shape, lambda *_: zeros, pipeline_mode=pl.Buffered(1))


class _LayerOf(NamedTuple):
    stack: jax.Array
    layer: int


def _layer_spec(w):
    shape = w.stack.shape
    index = (w.layer,) + (0,) * (len(shape) - 1)
    return pl.BlockSpec((None,) + shape[1:], lambda *_: index, pipeline_mode=pl.Buffered(1))


def _rows_spec(tm, width, per_row=True):
    if per_row:
        return pl.BlockSpec((None, tm, width), lambda b, i: (b, i, 0))
    return pl.BlockSpec((None, 1, width), lambda b, i: (b, 0, 0))


def _dot(a, b):
    return jnp.dot(a, b, preferred_element_type=F32)


def _dot_nt(a, b):
    return lax.dot_general(a, b, (((1,), (1,)), ((), ())), preferred_element_type=F32)


def _mod_rows(ref, rows):
    return ref[...] if ref.shape[0] == 1 else ref[rows, :]


def _rms_mod(x, g, shift, scale):
    ms = jnp.mean(x * x, axis=-1, keepdims=True)
    y = x * lax.rsqrt(ms + EPS) * g
    return y * (1.0 + scale) + shift


def _shr(x, pow2):
    shift = pow2.bit_length() - 1
    assert 1 << shift == pow2
    return jnp.right_shift(x, shift)


def _silu(x):
    return x / (1.0 + jnp.exp(-x))


def _gelu_tanh(x):
    c = math.sqrt(2.0 / math.pi)
    half = 0.5 * x
    return half + half * jnp.tanh(x * (c + (c * 0.044715) * (x * x)))


def _ffn(x1, ng, shf, scf, gf, wgu_ref, wd_ref):
    d_ff = wd_ref.shape[0]
    n_tiles, rem = divmod(d_ff, MXU_TILE)
    assert rem == 0
    split = (n_tiles + 1) // 2 * MXU_TILE
    h = _rms_mod(x1, ng, shf, scf).astype(BF16)
    acc = None
    for lo, hi in ((0, split), (split, d_ff)):
        g = _dot(h, wgu_ref[:, lo:hi])
        u = _dot(h, wgu_ref[:, d_ff + lo:d_ff + hi])
        a = (_silu(g) * u).astype(BF16)
        d = _dot(a, wd_ref[lo:hi, :])
        acc = d if acc is None else acc + d
    return x1 + gf * acc


def _adaln_kernel(c_ref, w_ref, b_ref, o_ref):
    s = _silu(c_ref[...])
    o_ref[...] = jnp.dot(s, w_ref[...], precision=lax.Precision.HIGHEST,
                         preferred_element_type=F32) + b_ref[...]


def _adaln(c_all, w_ada, b_ada):
    depth, d, d6 = w_ada.shape
    r = c_all.shape[0]
    tn = d6 // 4
    return pl.pallas_call(
        _adaln_kernel,
        grid=(depth, d6 // tn),
        in_specs=[
            pl.BlockSpec((r, d), lambda l, j: (0, 0)),
            pl.BlockSpec((None, d, tn), lambda l, j: (l, 0, j)),
            pl.BlockSpec((None, 1, tn), lambda l, j: (l, 0, j)),
        ],
        out_specs=pl.BlockSpec((None, r, tn), lambda l, j: (l, 0, j)),
        out_shape=jax.ShapeDtypeStruct((depth, r, d6), F32),
        compiler_params=_cparams(2),
        name="adaln",
    )(c_all, w_ada, b_ada.reshape(depth, 1, d6))


def _group_mean_sq(z, bd_ref):
    sq = (z * z).astype(BF16)
    w = bd_ref.shape[0]
    return jnp.concatenate([_dot(sq[:, c:c + w], bd_ref[...]) for c in range(0, z.shape[1], w)],
                           axis=1)


def _inproj_kernel(*refs, tm, n_sub, d_a, qk_w, has_fill, period):
    if has_fill:
        (x_ref, sh_ref, sc_ref, ng_ref, w_ref, cw_ref, qg_ref, kg_ref, bd_ref, f1_ref, f2_ref,
         a_ref, q_ref, k_ref, v_ref, kb_ref, vt_ref, tail_ref) = refs
    else:
        (x_ref, sh_ref, sc_ref, ng_ref, w_ref, cw_ref, qg_ref, kg_ref, bd_ref,
         a_ref, q_ref, k_ref, v_ref, kb_ref, vt_ref, tail_ref, carry_ref) = refs

    if not has_fill:
        @pl.when(pl.program_id(1) == 0)
        def _():
            carry_ref[...] = jnp.zeros_like(carry_ref)

    sub = tm // n_sub
    for r0 in range(0, tm, sub):
        rows = slice(r0, r0 + sub)
        h = _rms_mod(x_ref[rows, :], ng_ref[...], _mod_rows(sh_ref, rows),
                     _mod_rows(sc_ref, rows)).astype(BF16)

        a_b = _dot(h, w_ref[:, 0:d_a])
        xc = _dot(h, w_ref[:, d_a:2 * d_a]) * _dot(h, w_ref[:, 2 * d_a:3 * d_a])
        row = lax.broadcasted_iota(jnp.int32, (sub, d_a), 0)
        if has_fill:
            rp = jnp.bitwise_and(row, period - 1)
            fill1 = f1_ref[rows, :]
            fill2 = f2_ref[rows, :]
        else:
            rp = row
            last1 = carry_ref[pl.ds(7, 1), :]
            last2 = carry_ref[pl.ds(6, 1), :]
            fill1 = jnp.broadcast_to(last1, (sub, d_a))
            fill2 = jnp.where(row == 0, last2, last1)
        prev1 = jnp.where(rp < 1, fill1, pltpu.roll(xc, 1, 0))
        prev2 = jnp.where(rp < 2, fill2, pltpu.roll(xc, 2, 0))
        y = cw_ref[0:1, :] * prev2 + cw_ref[1:2, :] * prev1 + cw_ref[2:3, :] * xc
        a_ref[rows, :] = (a_b * y).astype(BF16)
        if has_fill:
            tail_ref[rows, :] = xc
        else:
            carry_ref[...] = xc[sub - 8:sub, :]
            if r0 + sub == tm:
                tail_ref[...] = xc[sub - 8:sub, :]

        q0 = 3 * d_a
        zq = _dot(h, w_ref[:, q0:q0 + qk_w])
        msq = _group_mean_sq(zq, bd_ref)
        qn = zq * lax.rsqrt(msq + EPS) * qg_ref[...]
        q_ref[rows, :] = (qn * Q_SCALE).astype(BF16)
        zk = _dot(h, w_ref[:, q0 + qk_w:q0 + 2 * qk_w])
        msk = _group_mean_sq(zk, bd_ref)
        kn = zk * lax.rsqrt(msk + EPS) * kg_ref[...]
        kb_ref[rows, :] = kn.astype(BF16)
        zv = _dot(h, w_ref[:, q0 + 2 * qk_w:])
        for out_ref, val in ((k_ref, kn), (v_ref, zv)):
            if has_fill:
                out_ref[rows, :] = val
            else:
                heads = val.shape[1] // DV
                for hh in range(heads):
                    out_ref[pl.ds(r0 * heads + hh, sub, stride=heads), :] = val[:, hh * DV:(hh + 1) * DV]
        vt_ref[:, rows] = zv.T.astype(BF16)


def _inproj(x, sh, sc, ng, w_in, cw, qg, kg, bd, fills, tm):
    nb, t, d = x.shape
    d_a = cw.shape[1]
    qk_w = qg.shape[1]
    d_b = w_in.shape[1] - 3 * d_a - 2 * qk_w
    has_fill = fills is not None
    per_row = sh.shape[1] != 1
    tail_rows = tm if has_fill else 8
    in_specs = [
        _rows_spec(tm, d), _rows_spec(tm, d, per_row), _rows_spec(tm, d, per_row),
        _const_spec(ng.shape), _const_spec(w_in.shape), _const_spec(cw.shape),
        _const_spec(qg.shape), _const_spec(kg.shape), _const_spec(bd.shape),
    ]
    args = [x, sh, sc, ng, w_in, cw, qg, kg, bd]
    scratch = []
    if has_fill:
        in_specs += [_rows_spec(tm, d_a), _rows_spec(tm, d_a)]
        args += list(fills)
        tail_spec = _rows_spec(tm, d_a)
        tail_shape = (nb, t, d_a)
    else:
        scratch = [pltpu.VMEM((8, d_a), F32)]
        tail_spec = pl.BlockSpec((None, 8, d_a), lambda b, i: (b, 0, 0))
        tail_shape = (nb, 8, d_a)
    if has_fill:
        kv_specs = [_rows_spec(tm, qk_w), _rows_spec(tm, d_b)]
        kv_shapes = [(nb, t, qk_w), (nb, t, d_b)]
    else:
        kv_specs = [_rows_spec(tm * (qk_w // DV), DV), _rows_spec(tm * (d_b // DV), DV)]
        kv_shapes = [(nb, t * (qk_w // DV), DV), (nb, t * (d_b // DV), DV)]
    out_specs = [_rows_spec(tm, d_a), _rows_spec(tm, qk_w), *kv_specs,
                 _rows_spec(tm, qk_w), pl.BlockSpec((None, d_b, tm), lambda b, i: (b, 0, i)), tail_spec]
    out_shape = [
        jax.ShapeDtypeStruct((nb, t, d_a), BF16), jax.ShapeDtypeStruct((nb, t, qk_w), BF16),
        jax.ShapeDtypeStruct(kv_shapes[0], F32), jax.ShapeDtypeStruct(kv_shapes[1], F32),
        jax.ShapeDtypeStruct((nb, t, qk_w), BF16), jax.ShapeDtypeStruct((nb, d_b, t), BF16),
        jax.ShapeDtypeStruct(tail_shape, F32),
    ]
    return pl.pallas_call(
        functools.partial(_inproj_kernel, tm=tm, n_sub=1 if has_fill else INPROJ_SUBTILES, d_a=d_a,
                          qk_w=qk_w, has_fill=has_fill, period=8),
        grid=(nb, t // tm),
        in_specs=in_specs, out_specs=out_specs, out_shape=out_shape,
        scratch_shapes=scratch,
        compiler_params=_cparams(2),
        name="inproj",
    )(*args)


def _diff_lambda(lam_ref, lam_init):
    lv = lam_ref[...]
    e1 = jnp.exp(jnp.sum(lv[0:1, :] * lv[1:2, :], axis=-1, keepdims=True))
    e2 = jnp.exp(jnp.sum(lv[2:3, :] * lv[3:4, :], axis=-1, keepdims=True))
    return e1 - e2 + lam_init


def _subln(o, g, lam_init):
    ms = jnp.mean(o * o, axis=-1, keepdims=True)
    return o * lax.rsqrt(ms + EPS) * g * (1.0 - lam_init)


def _softmax_step(s, v_dot, m_ref, l_ref, acc_ref, rows=slice(None)):
    m_prev = m_ref[rows, :]
    m_new = jnp.maximum(m_prev, jnp.max(s, axis=-1, keepdims=True))
    alpha = jnp.exp2(m_prev - m_new)
    p = jnp.exp2(s - m_new)
    l_ref[rows, :] = alpha * l_ref[rows, :] + jnp.sum(p, axis=-1, keepdims=True)
    acc_ref[rows, :] = alpha * acc_ref[rows, :] + v_dot(p.astype(BF16))
    m_ref[rows, :] = m_new


def _attn_kernel(q_ref, k_ref, vt_ref, lam_ref, sg_ref, o_ref, qs_ref, sa_ref, sb_ref, m_ref, acc_ref,
                 *, tq, hps, lam_init):
    i = pl.program_id(2)
    lane = lax.broadcasted_iota(jnp.int32, (tq, DV), 1)
    for hh in range(hps):
        q = q_ref[:, hh * DV:(hh + 1) * DV]
        zero = jnp.zeros_like(q)
        qs_ref[hh, 0:tq, :] = jnp.where(lane < DK, q, zero)
        qs_ref[hh, tq:2 * tq, :] = jnp.where(lane >= DK, q, zero)
    m_ref[...] = jnp.full_like(m_ref, -jnp.inf)
    acc_ref[...] = jnp.zeros_like(acc_ref)

    chains = [(hh, c0, hh * 2 * tq + c0) for hh in range(hps) for c0 in range(0, 2 * tq, ATTN_COLS)]
    ones = jnp.ones((acc_ref.shape[0] - DV, tq), BF16)

    def scores(j, s_ref):
        keys = pl.ds(pl.multiple_of(j * tq, tq), tq)
        for hh, c0, g0 in chains:
            k = k_ref[keys, hh * DV:(hh + 1) * DV]
            s_ref[:, g0:g0 + ATTN_COLS] = _dot_nt(k, qs_ref[hh, c0:c0 + ATTN_COLS, :])

    def step(j, s_ref, diagonal):
        keys = pl.ds(pl.multiple_of(j * tq, tq), tq)
        vts = [jnp.concatenate([vt_ref[hh * DV:(hh + 1) * DV, keys], ones], axis=0)
               for hh in range(hps)]
        for hh, c0, g0 in chains:
            cols = slice(g0, g0 + ATTN_COLS)
            st = s_ref[:, cols]
            if diagonal:
                key = lax.broadcasted_iota(jnp.int32, st.shape, 0)
                qrow = lax.broadcasted_iota(jnp.int32, st.shape, 1) + (c0 % tq)
                st = jnp.where(key <= qrow, st, -jnp.inf)
            m_prev = m_ref[:, cols]
            m_new = jnp.maximum(m_prev, jnp.max(st, axis=0, keepdims=True))
            alpha = jnp.exp2(m_prev - m_new)
            pt = jnp.exp2(st - m_new)
            acc_ref[:, cols] = alpha * acc_ref[:, cols] + _dot(vts[hh], pt.astype(BF16))
            m_ref[:, cols] = m_new

    scores(0, sa_ref)

    def pair(jj, carry):
        j0 = 2 * jj
        scores(j0 + 1, sb_ref)
        step(j0, sa_ref, False)
        scores(j0 + 2, sa_ref)
        step(j0 + 1, sb_ref, False)
        return carry

    lax.fori_loop(0, i // 2, pair, 0)

    @pl.when(i % 2 == 0)
    def _():
        step(i, sa_ref, True)

    @pl.when(i % 2 == 1)
    def _():
        scores(i, sb_ref)
        step(i - 1, sa_ref, False)
        step(i, sb_ref, True)

    lam = _diff_lambda(lam_ref, lam_init)
    for hh in range(hps):
        h0 = hh * 2 * tq
        nt = acc_ref[0:DV, h0:h0 + 2 * tq] / acc_ref[DV:DV + 1, h0:h0 + 2 * tq]
        ot = nt[:, 0:tq] - lam * nt[:, tq:2 * tq]
        ms = jnp.mean(ot * ot, axis=0, keepdims=True)
        ot = ot * lax.rsqrt(ms + EPS)
        o_ref[:, hh * DV:(hh + 1) * DV] = (ot.T * sg_ref[...] * (1.0 - lam_init)).astype(o_ref.dtype)


def _attn(q, kb, vt, lamv, sg, lam_init, tq, hps):
    b, t, w = q.shape
    hw = hps * DV
    return pl.pallas_call(
        functools.partial(_attn_kernel, tq=tq, hps=hps, lam_init=lam_init),
        grid=(b, w // hw, t // tq),
        in_specs=[
            pl.BlockSpec((None, tq, hw), lambda b_, h, i: (b_, i, h)),
            pl.BlockSpec((None, t, hw), lambda b_, h, i: (b_, 0, h)),
            pl.BlockSpec((None, hw, t), lambda b_, h, i: (b_, h, 0)),
            _const_spec(lamv.shape), _const_spec(sg.shape),
        ],
        out_specs=pl.BlockSpec((None, tq, hw), lambda b_, h, i: (b_, i, h)),
        out_shape=jax.ShapeDtypeStruct((b, t, w), BF16),
        scratch_shapes=[
            pltpu.VMEM((hps, 2 * tq, DV), BF16),
            pltpu.VMEM((tq, hps * 2 * tq), F32), pltpu.VMEM((tq, hps * 2 * tq), F32),
            pltpu.VMEM((1, hps * 2 * tq), F32),
            pltpu.VMEM((DV + BF16_ROWS, hps * 2 * tq), F32),
        ],
        compiler_params=_cparams(3),
        name="attn",
    )(q, kb, vt, lamv, sg)


def _sattn_kernel(pt_ref, q_ref, kn_ref, vn_ref, lam_ref, sg_ref, *rest, pp, ts, lam_init):
    k_refs = rest[0:pp]
    v_refs = rest[pp:2 * pp]
    o_ref, qs_ref, m_ref, l_ref, acc_ref = rest[2 * pp:]
    j = pl.program_id(1)
    rows = qs_ref.shape[0]
    hr = 2 * ts
    heads = rows // hr
    page = k_refs[0].shape[0] // heads

    @pl.when(j == 0)
    def _():
        q = q_ref[...]
        lane = lax.broadcasted_iota(jnp.int32, (ts, DV), 1)
        parts = []
        for h in range(heads):
            qh = q[:, h * DV:(h + 1) * DV]
            parts += [jnp.where(lane < DK, qh, 0.0), jnp.where(lane >= DK, qh, 0.0)]
        qs = jnp.concatenate(parts, axis=0).astype(BF16)
        qs_ref[...] = qs
        m_ref[...] = jnp.full_like(m_ref, -jnp.inf)
        l_ref[...] = jnp.zeros_like(l_ref)
        acc_ref[...] = jnp.zeros_like(acc_ref)
        pad = jnp.zeros((DV - ts, DV), F32)
        kns = [jnp.concatenate([kn_ref[:, h * DV:(h + 1) * DV], pad], axis=0).astype(BF16)
               for h in range(heads)]
        vns = [jnp.concatenate([vn_ref[:, h * DV:(h + 1) * DV], pad], axis=0).astype(BF16)
               for h in range(heads)]
        s = jnp.concatenate([_dot_nt(qs[h * hr:(h + 1) * hr, :], kns[h]) for h in range(heads)],
                            axis=0)
        rr = lax.broadcasted_iota(jnp.int32, s.shape, 0)
        cc = lax.broadcasted_iota(jnp.int32, s.shape, 1)
        s = jnp.where(cc <= jnp.bitwise_and(rr, ts - 1), s, -jnp.inf)

        def new_v_dot(pb):
            return jnp.concatenate([_dot(pb[h * hr:(h + 1) * hr, :], vns[h]) for h in range(heads)],
                                   axis=0)

        _softmax_step(s, new_v_dot, m_ref, l_ref, acc_ref)

    def head_rows(refs, h):
        return jnp.concatenate([r[pl.ds(h, page, stride=heads), :] for r in refs],
                               axis=0).astype(BF16)

    hrows = [slice(h * hr, (h + 1) * hr) for h in range(heads)]
    ss = [_dot_nt(qs_ref[hrows[h], :], head_rows(k_refs, h)) for h in range(heads)]
    for h in range(heads):
        _softmax_step(ss[h], lambda pb: _dot(pb, head_rows(v_refs, h)), m_ref, l_ref, acc_ref,
                      hrows[h])

    @pl.when(j == pl.num_programs(1) - 1)
    def _():
        lam = _diff_lambda(lam_ref, lam_init)
        n = acc_ref[...] / l_ref[...]
        outs = []
        for h in range(heads):
            o = n[h * hr:h * hr + ts, :] - lam * n[h * hr + ts:(h + 1) * hr, :]
            outs.append(_subln(o, sg_ref[...], lam_init))
        o_ref[...] = jnp.concatenate(outs, axis=1)


def _sattn(page_table, q, kn, vn, lamv, sg, cache_k, cache_v, layer, lam_init, pp):
    bs, ts, w = q.shape
    n_pages = page_table.shape[1]
    prow = cache_k.shape[2] * cache_k.shape[3]
    ck = cache_k.reshape(cache_k.shape[0], cache_k.shape[1], prow, DV)
    cv = cache_v.reshape(cache_v.shape[0], cache_v.shape[1], prow, DV)
    rows = (w // DK) * ts

    def page_spec(p):
        return pl.BlockSpec((None, None, prow, DV),
                            lambda b, j, pt: (layer, pt[b, j * pp + p], 0, 0))

    new_spec = pl.BlockSpec((None, ts, w), lambda b, j, pt: (b, 0, 0))
    grid_spec = pltpu.PrefetchScalarGridSpec(
        num_scalar_prefetch=1,
        grid=(bs, n_pages // pp),
        in_specs=[new_spec, new_spec, new_spec, _const_spec(lamv.shape), _const_spec(sg.shape)]
        + [page_spec(p) for p in range(pp)] + [page_spec(p) for p in range(pp)],
        out_specs=new_spec,
        scratch_shapes=[
            pltpu.VMEM((rows, DV), BF16),
            pltpu.VMEM((rows, 1), F32), pltpu.VMEM((rows, 1), F32),
            pltpu.VMEM((rows, DV), F32),
        ],
    )
    return pl.pallas_call(
        functools.partial(_sattn_kernel, pp=pp, ts=ts, lam_init=lam_init),
        grid_spec=grid_spec,
        out_shape=jax.ShapeDtypeStruct((bs, ts, w), F32),
        compiler_params=_cparams(2),
        name="sattn",
    )(page_table, q, kn, vn, lamv, sg, *([ck] * pp), *([cv] * pp))


def _post_even_kernel(x_ref, a_ref, o_ref, g_ref, wo_ref, ng_ref, shf_ref, scf_ref, gf_ref,
                      wgu_ref, wd_ref, y_ref):
    d_a = a_ref.shape[1]
    m = _dot(a_ref[...], wo_ref[0:d_a, :]) + _dot(o_ref[...].astype(BF16), wo_ref[d_a:, :])
    x1 = x_ref[...] + g_ref[...] * m
    y_ref[...] = _ffn(x1, ng_ref[...], shf_ref[...], scf_ref[...], gf_ref[...], wgu_ref, wd_ref)


def _post_even(x, a, o, g, w_out, ng, shf, scf, gf, wgu, wd, tm):
    nb, t, d = x.shape
    per_row = g.shape[1] != 1
    mod = _rows_spec(tm, d, per_row)
    return pl.pallas_call(
        _post_even_kernel,
        grid=(nb, t // tm),
        in_specs=[
            _rows_spec(tm, d), _rows_spec(tm, a.shape[2]), _rows_spec(tm, o.shape[2]), mod,
            _const_spec(w_out.shape), _const_spec(ng.shape), mod, mod, mod,
            _layer_spec(wgu), _layer_spec(wd),
        ],
        out_specs=_rows_spec(tm, d),
        out_shape=jax.ShapeDtypeStruct((nb, t, d), F32),
        compiler_params=_cparams(2),
        name="post_even",
    )(x, a, o, g, w_out, ng, shf, scf, gf, wgu.stack, wd.stack)


def _odd_kernel(*refs, tm, short_rows, emit_v):
    (x_ref, sh_ref, sc_ref, ng0_ref, wi_ref, bi_ref, lg_ref, lb_ref, ws_ref, bimg_ref, wo_ref,
     g_ref, ng1_ref, shf_ref, scf_ref, gf_ref, wgu_ref, wd_ref) = refs[:18]
    if emit_v:
        y_ref, v_ref, mix_ref = refs[18:]
    else:
        y_ref, mix_ref = refs[18:]
    d_c = wo_ref.shape[0]
    gw = d_c // G_C

    x = x_ref[...]
    h = _rms_mod(x, ng0_ref[...], sh_ref[...], sc_ref[...]).astype(BF16)
    u = _gelu_tanh(_dot(h, wi_ref[:, 0:d_c]) + bi_ref[:, 0:d_c])
    v = _gelu_tanh(_dot(h, wi_ref[:, d_c:]) + bi_ref[:, d_c:])
    mu = jnp.mean(v, axis=-1, keepdims=True)
    vc = v - mu
    var = jnp.mean(vc * vc, axis=-1, keepdims=True)
    vn = vc * lax.rsqrt(var + EPS) * lg_ref[...] + lb_ref[...]
    if emit_v:
        v_ref[...] = vn
    vb = vn.astype(BF16)

    r = lax.broadcasted_iota(jnp.int32, (CHUNK, CHUNK), 0)
    c = lax.broadcasted_iota(jnp.int32, (CHUNK, CHUNK), 1)
    if short_rows is None:
        for gi in range(G_C):
            wsg = jnp.where(c <= r, ws_ref[gi], 0.0).astype(BF16)
            for n in range(tm // CHUNK):
                blk = _dot(wsg, vb[n * CHUNK:(n + 1) * CHUNK, gi * gw:(gi + 1) * gw])
                mix_ref[n * CHUNK:(n + 1) * CHUNK, gi * gw:(gi + 1) * gw] = (
                    blk + bimg_ref[:, gi * gw:(gi + 1) * gw])
    else:
        rr = lax.broadcasted_iota(jnp.int32, (tm, CHUNK), 0)
        cc = lax.broadcasted_iota(jnp.int32, (tm, CHUNK), 1)
        sel = jnp.where(jnp.bitwise_and(rr, short_rows - 1) == cc, 1.0, 0.0).astype(BF16)
        r2 = lax.broadcasted_iota(jnp.int32, (tm, tm), 0)
        c2 = lax.broadcasted_iota(jnp.int32, (tm, tm), 1)
        same_seq = _shr(r2, short_rows) == _shr(c2, short_rows)
        causal = jnp.bitwise_and(c2, short_rows - 1) <= jnp.bitwise_and(r2, short_rows - 1)
        keep = jnp.where(same_seq, jnp.where(causal, 1.0, 0.0), 0.0)
        for gi in range(G_C):
            spread = _dot_nt(_dot(sel, ws_ref[gi].astype(BF16)).astype(BF16), sel)
            mg = (spread * keep).astype(BF16)
            blk = _dot(mg, vb[:, gi * gw:(gi + 1) * gw])
            mix_ref[:, gi * gw:(gi + 1) * gw] = blk + bimg_ref[:, gi * gw:(gi + 1) * gw]

    y = _dot((u * mix_ref[...]).astype(BF16), wo_ref[...])
    x1 = x + g_ref[...] * y
    y_ref[...] = _ffn(x1, ng1_ref[...], shf_ref[...], scf_ref[...], gf_ref[...], wgu_ref, wd_ref)


def _odd(x, sh, sc, ng0, w_in, b_in, ln_g, ln_b, w_s, bimg, w_out, g, ng1, shf, scf, gf, wgu, wd,
         tm, short_rows, emit_v):
    nb, t, d = x.shape
    d_c = w_out.shape[0]
    per_row = sh.shape[1] != 1
    mod = _rows_spec(tm, d, per_row)
    out_specs = [_rows_spec(tm, d)]
    out_shape = [jax.ShapeDtypeStruct((nb, t, d), F32)]
    if emit_v:
        out_specs.append(_rows_spec(tm, d_c))
        out_shape.append(jax.ShapeDtypeStruct((nb, t, d_c), F32))
    return pl.pallas_call(
        functools.partial(_odd_kernel, tm=tm, short_rows=short_rows, emit_v=emit_v),
        grid=(nb, t // tm),
        in_specs=[
            _rows_spec(tm, d), mod, mod, _const_spec(ng0.shape), _const_spec(w_in.shape),
            _const_spec(b_in.shape), _const_spec(ln_g.shape), _const_spec(ln_b.shape),
            _const_spec(w_s.shape), _const_spec(bimg.shape), _const_spec(w_out.shape),
            mod, _const_spec(ng1.shape), mod, mod, mod,
            _layer_spec(wgu), _layer_spec(wd),
        ],
        out_specs=out_specs, out_shape=out_shape,
        scratch_shapes=[pltpu.VMEM((tm, d_c), F32)],
        compiler_params=_cparams(2),
        name="odd",
    )(x, sh, sc, ng0, w_in, b_in, ln_g, ln_b, w_s, bimg, w_out, g, ng1, shf, scf, gf, wgu.stack, wd.stack)


def kernel(x_prompt, x_sample, c_prompt, c_sample, cache_k, cache_v, state_conv, page_table, w_ada, b_ada, norm_g, w_in_even, conv_w, q_norm_g, k_norm_g, lam_q1, lam_k1, lam_q2, lam_k2, subln_g, w_out_even, w_in_odd, b_in_odd, ln_g, ln_b, w_s, b_s, w_out_odd, w_gate_up, w_down):
    bp, tp, d = x_prompt.shape
    bs, ts, _ = x_sample.shape
    depth = w_ada.shape[0]
    d_a = conv_w.shape[2]
    qk_w = H_B * 2 * DK
    rows_s = bs * ts

    mods = _adaln(jnp.concatenate([c_prompt, c_sample], axis=0), w_ada, b_ada)

    grp = jnp.arange(MXU_TILE) // DK
    bd = jnp.where(grp[:, None] == grp[None, :], 1.0 / DK, 0.0).astype(BF16)

    wgu_all = w_gate_up.astype(BF16)
    wd_all = w_down.astype(BF16)
    xp = x_prompt
    xs = x_sample.reshape(1, rows_s, d)
    k_p, v_p, conv_p, k_s, v_s, conv_s, chunk_v = [], [], [], [], [], [], []

    for i in range(depth):
        mp = [m.reshape(bp, 1, d) for m in jnp.split(mods[i, :bp], 6, axis=-1)]
        ms = [jnp.repeat(m, ts, axis=0).reshape(1, rows_s, d)
              for m in jnp.split(mods[i, bp:], 6, axis=-1)]
        ng0 = norm_g[i, 0].reshape(1, d)
        ng1 = norm_g[i, 1].reshape(1, d)
        wgu = _LayerOf(wgu_all, i)
        wd = _LayerOf(wd_all, i)
        if i % 2 == 0:
            e = i // 2
            lam_init = 0.8 - 0.6 * math.exp(-0.3 * i)
            w_in = w_in_even[e].astype(BF16)
            w_out = w_out_even[e].astype(BF16)
            qg = jnp.tile(q_norm_g[e], qk_w // DK).reshape(1, qk_w)
            kg = jnp.tile(k_norm_g[e], qk_w // DK).reshape(1, qk_w)
            lamv = jnp.stack([lam_q1[e], lam_k1[e], lam_q2[e], lam_k2[e]])
            sg = subln_g[e].reshape(1, DV)

            a, q, k, v, kb, vt, tail = _inproj(xp, mp[0], mp[1], ng0, w_in, conv_w[e], qg, kg, bd,
                                               None, ROW_TILE)
            o = _attn(q, kb, vt, lamv, sg, lam_init, ATTN_TILE, ATTN_HEADS)
            xp = _post_even(xp, a, o, mp[2], w_out, ng1, mp[3], mp[4], mp[5], wgu, wd, ROW_TILE)
            k_p.append(k.reshape(bp, tp, H_B, 2 * DK))
            v_p.append(v.reshape(bp, tp, H_B, DV))
            conv_p.append(tail[:, 8 - (CONV_W - 1):, :])

            st = state_conv[e]
            zeros = jnp.zeros((bs, ts - 1, d_a), F32)
            fill1 = jnp.concatenate([st[:, 1:2], zeros], axis=1).reshape(1, rows_s, d_a)
            fill2 = jnp.concatenate([st, zeros[:, 1:]], axis=1).reshape(1, rows_s, d_a)
            a, q, k, v, _, _, xc = _inproj(xs, ms[0], ms[1], ng0, w_in, conv_w[e], qg, kg, bd,
                                           (fill1, fill2), rows_s)
            o = _sattn(page_table, q.astype(F32).reshape(bs, ts, qk_w), k.reshape(bs, ts, qk_w),
                       v.reshape(bs, ts, -1), lamv, sg, cache_k, cache_v, e, lam_init,
                       PAGES_PER_STEP)
            xs = _post_even(xs, a, o.reshape(1, rows_s, -1), ms[2], w_out, ng1, ms[3], ms[4], ms[5],
                            wgu, wd, rows_s)
            k_s.append(k.reshape(bs, ts, H_B, 2 * DK))
            v_s.append(v.reshape(bs, ts, H_B, DV))
            conv_s.append(xc.reshape(bs, ts, d_a)[:, ts - (CONV_W - 1):, :])
        else:
            oi = i // 2
            d_c = w_out_odd.shape[1]
            w_in = w_in_odd[oi].astype(BF16)
            w_out = w_out_odd[oi].astype(BF16)
            b_in = b_in_odd[oi].reshape(1, -1)
            lg = ln_g[oi].reshape(1, d_c)
            lb = ln_b[oi].reshape(1, d_c)
            bimg = jnp.repeat(b_s[oi].T, d_c // G_C, axis=1)
            xp = _odd(xp, mp[0], mp[1], ng0, w_in, b_in, lg, lb, w_s[oi], bimg, w_out, mp[2], ng1,
                      mp[3], mp[4], mp[5], wgu, wd, ROW_TILE, None, False)[0]
            bimg_s = jnp.tile(bimg[:ts], (bs, 1))
            xs, vrows = _odd(xs, ms[0], ms[1], ng0, w_in, b_in, lg, lb, w_s[oi], bimg_s, w_out, ms[2],
                             ng1, ms[3], ms[4], ms[5], wgu, wd, rows_s, ts, True)
            chunk_v.append(vrows.reshape(bs, ts, d_c))

    return (xp, xs.reshape(bs, ts, d), jnp.stack(k_p), jnp.stack(v_p), jnp.stack(conv_p),
            jnp.stack(k_s), jnp.stack(v_s), jnp.stack(conv_s), jnp.stack(chunk_v))
```

```python
import functools
import math
from typing import NamedTuple

import jax
import jax.numpy as jnp
from jax import lax
from jax.experimental import pallas as pl
from jax.experimental.pallas import tpu as pltpu

F32 = jnp.float32
BF16 = jnp.bfloat16
EPS = 1e-6

A_GROUPS = 8
CONV_W = 3
H_B = 4
DK = 64
DV = 2 * DK
CHUNK = 128
G_C = 8
Q_SCALE = DK ** -0.5 * math.log2(math.e)

VMEM_LIMIT_BYTES = 56 * 1024 * 1024
MXU_TILE = 256
BF16_ROWS = 16

ROW_TILE = 512
INPROJ_SUBTILES = 2
ATTN_TILE = 512
ATTN_COLS = 512
ATTN_HEADS = 4
PAGES_PER_STEP = 32


def _cparams(n_axes):
    return pltpu.CompilerParams(
        dimension_semantics=("arbitrary",) * n_axes,
        vmem_limit_bytes=VMEM_LIMIT_BYTES)


def _const_spec(shape):
    zeros = (0,) * len(shape)
    return pl.BlockSpec(shape, lambda *_: zeros, pipeline_mode=pl.Buffered(1))


class _LayerOf(NamedTuple):
    stack: jax.Array
    layer: int


def _layer_spec(w):
    shape = w.stack.shape
    index = (w.layer,) + (0,) * (len(shape) - 1)
    return pl.BlockSpec((None,) + shape[1:], lambda *_: index, pipeline_mode=pl.Buffered(1))


class _ModChunk(NamedTuple):
    mods: jax.Array
    chunk: int


def _mod_spec(m, tm, d):
    chunk = m.chunk
    if m.mods.shape[1] == 1:
        return pl.BlockSpec((None, 1, d), lambda b, i: (b, 0, chunk))
    return pl.BlockSpec((None, tm, d), lambda b, i: (b, i, chunk))


def _rows_spec(tm, width):
    return pl.BlockSpec((None, tm, width), lambda b, i: (b, i, 0))


def _dot(a, b):
    return jnp.dot(a, b, preferred_element_type=F32)


def _dot_nt(a, b):
    return lax.dot_general(a, b, (((1,), (1,)), ((), ())), preferred_element_type=F32)


def _mod_rows(ref, rows):
    return ref[...] if ref.shape[0] == 1 else ref[rows, :]


def _rms_mod(x, g, shift, scale):
    ms = jnp.mean(x * x, axis=-1, keepdims=True)
    y = x * lax.rsqrt(ms + EPS) * g
    return y * (1.0 + scale) + shift


def _shr(x, pow2):
    shift = pow2.bit_length() - 1
    assert 1 << shift == pow2
    return jnp.right_shift(x, shift)


def _silu(x):
    return x / (1.0 + jnp.exp(-x))


def _gelu_tanh(x):
    c = math.sqrt(2.0 / math.pi)
    half = 0.5 * x
    return half + half * jnp.tanh(x * (c + (c * 0.044715) * (x * x)))


def _ffn(x1, ng, shf, scf, gf, wgu_ref, wd_ref):
    d_ff = wd_ref.shape[0]
    n_tiles, rem = divmod(d_ff, MXU_TILE)
    assert rem == 0
    split = (n_tiles + 1) // 2 * MXU_TILE
    h = _rms_mod(x1, ng, shf, scf).astype(BF16)
    acc = None
    for lo, hi in ((0, split), (split, d_ff)):
        g = _dot(h, wgu_ref[:, lo:hi])
        u = _dot(h, wgu_ref[:, d_ff + lo:d_ff + hi])
        a = (_silu(g) * u).astype(BF16)
        d = _dot(a, wd_ref[lo:hi, :])
        acc = d if acc is None else acc + d
    return x1 + gf * acc


def _adaln_kernel(c_ref, w_ref, b_ref, o_ref):
    s = _silu(c_ref[...])
    o_ref[...] = jnp.dot(s, w_ref[...], precision=lax.Precision.HIGHEST,
                         preferred_element_type=F32) + b_ref[...]


def _adaln(c_all, w_ada, b_ada):
    depth, d, d6 = w_ada.shape
    r = c_all.shape[0]
    tn = d6 // 4
    return pl.pallas_call(
        _adaln_kernel,
        grid=(depth, d6 // tn),
        in_specs=[
            pl.BlockSpec((r, d), lambda l, j: (0, 0)),
            pl.BlockSpec((None, d, tn), lambda l, j: (l, 0, j)),
            pl.BlockSpec((None, 1, tn), lambda l, j: (l, 0, j)),
        ],
        out_specs=pl.BlockSpec((None, r, tn), lambda l, j: (l, 0, j)),
        out_shape=jax.ShapeDtypeStruct((depth, r, d6), F32),
        compiler_params=_cparams(2),
        name="adaln",
    )(c_all, w_ada, b_ada.reshape(depth, 1, d6))


def _group_mean_sq(z, bd_ref):
    sq = (z * z).astype(BF16)
    w = bd_ref.shape[0]
    return jnp.concatenate([_dot(sq[:, c:c + w], bd_ref[...]) for c in range(0, z.shape[1], w)],
                           axis=1)


def _inproj_kernel(*refs, tm, n_sub, d_a, qk_w, has_fill, period):
    if has_fill:
        (x_ref, sh_ref, sc_ref, ng_ref, w_ref, cw_ref, qg_ref, kg_ref, bd_ref, f1_ref, f2_ref,
         a_ref, q_ref, k_ref, v_ref, kb_ref, vt_ref, tail_ref) = refs
    else:
        (x_ref, sh_ref, sc_ref, ng_ref, w_ref, cw_ref, qg_ref, kg_ref, bd_ref,
         a_ref, q_ref, k_ref, v_ref, kb_ref, vt_ref, tail_ref, carry_ref) = refs

    if not has_fill:
        @pl.when(pl.program_id(1) == 0)
        def _():
            carry_ref[...] = jnp.zeros_like(carry_ref)

    sub = tm // n_sub
    for r0 in range(0, tm, sub):
        rows = slice(r0, r0 + sub)
        h = _rms_mod(x_ref[rows, :], ng_ref[...], _mod_rows(sh_ref, rows),
                     _mod_rows(sc_ref, rows)).astype(BF16)

        a_b = _dot(h, w_ref[:, 0:d_a])
        xc = _dot(h, w_ref[:, d_a:2 * d_a]) * _dot(h, w_ref[:, 2 * d_a:3 * d_a])
        row = lax.broadcasted_iota(jnp.int32, (sub, d_a), 0)
        if has_fill:
            rp = jnp.bitwise_and(row, period - 1)
            fill1 = f1_ref[rows, :]
            fill2 = f2_ref[rows, :]
        else:
            rp = row
            last1 = carry_ref[pl.ds(7, 1), :]
            last2 = carry_ref[pl.ds(6, 1), :]
            fill1 = jnp.broadcast_to(last1, (sub, d_a))
            fill2 = jnp.where(row == 0, last2, last1)
        prev1 = jnp.where(rp < 1, fill1, pltpu.roll(xc, 1, 0))
        prev2 = jnp.where(rp < 2, fill2, pltpu.roll(xc, 2, 0))
        y = cw_ref[0:1, :] * prev2 + cw_ref[1:2, :] * prev1 + cw_ref[2:3, :] * xc
        a_ref[rows, :] = (a_b * y).astype(BF16)
        if has_fill:
            tail_ref[rows, :] = xc
        else:
            carry_ref[...] = xc[sub - 8:sub, :]
            if r0 + sub == tm:
                tail_ref[...] = xc[sub - 8:sub, :]

        q0 = 3 * d_a
        zq = _dot(h, w_ref[:, q0:q0 + qk_w])
        msq = _group_mean_sq(zq, bd_ref)
        qn = zq * lax.rsqrt(msq + EPS) * qg_ref[...]
        q_ref[rows, :] = (qn * Q_SCALE).astype(BF16)
        zk = _dot(h, w_ref[:, q0 + qk_w:q0 + 2 * qk_w])
        msk = _group_mean_sq(zk, bd_ref)
        kn = zk * lax.rsqrt(msk + EPS) * kg_ref[...]
        kb_ref[rows, :] = kn.astype(BF16)
        zv = _dot(h, w_ref[:, q0 + 2 * qk_w:])
        for out_ref, val in ((k_ref, kn), (v_ref, zv)):
            if has_fill:
                out_ref[rows, :] = val
            else:
                heads = val.shape[1] // DV
                for hh in range(heads):
                    out_ref[pl.ds(r0 * heads + hh, sub, stride=heads), :] = val[:, hh * DV:(hh + 1) * DV]
        vt_ref[:, rows] = zv.T.astype(BF16)


def _inproj(x, sh, sc, ng, w_in, cw, qg, kg, bd, fills, tm):
    nb, t, d = x.shape
    d_a = cw.shape[1]
    qk_w = qg.shape[1]
    d_b = w_in.shape[1] - 3 * d_a - 2 * qk_w
    has_fill = fills is not None
    in_specs = [
        _rows_spec(tm, d), _mod_spec(sh, tm, d), _mod_spec(sc, tm, d),
        _const_spec(ng.shape), _const_spec(w_in.shape), _const_spec(cw.shape),
        _const_spec(qg.shape), _const_spec(kg.shape), _const_spec(bd.shape),
    ]
    args = [x, sh.mods, sc.mods, ng, w_in, cw, qg, kg, bd]
    scratch = []
    if has_fill:
        in_specs += [_rows_spec(tm, d_a), _rows_spec(tm, d_a)]
        args += list(fills)
        tail_spec = _rows_spec(tm, d_a)
        tail_shape = (nb, t, d_a)
    else:
        scratch = [pltpu.VMEM((8, d_a), F32)]
        tail_spec = pl.BlockSpec((None, 8, d_a), lambda b, i: (b, 0, 0))
        tail_shape = (nb, 8, d_a)
    if has_fill:
        kv_specs = [_rows_spec(tm, qk_w), _rows_spec(tm, d_b)]
        kv_shapes = [(nb, t, qk_w), (nb, t, d_b)]
    else:
        kv_specs = [_rows_spec(tm * (qk_w // DV), DV), _rows_spec(tm * (d_b // DV), DV)]
        kv_shapes = [(nb, t * (qk_w // DV), DV), (nb, t * (d_b // DV), DV)]
    out_specs = [_rows_spec(tm, d_a), _rows_spec(tm, qk_w), *kv_specs,
                 _rows_spec(tm, qk_w), pl.BlockSpec((None, d_b, tm), lambda b, i: (b, 0, i)), tail_spec]
    out_shape = [
        jax.ShapeDtypeStruct((nb, t, d_a), BF16), jax.ShapeDtypeStruct((nb, t, qk_w), BF16),
        jax.ShapeDtypeStruct(kv_shapes[0], F32), jax.ShapeDtypeStruct(kv_shapes[1], F32),
        jax.ShapeDtypeStruct((nb, t, qk_w), BF16), jax.ShapeDtypeStruct((nb, d_b, t), BF16),
        jax.ShapeDtypeStruct(tail_shape, F32),
    ]
    return pl.pallas_call(
        functools.partial(_inproj_kernel, tm=tm, n_sub=1 if has_fill else INPROJ_SUBTILES, d_a=d_a,
                          qk_w=qk_w, has_fill=has_fill, period=8),
        grid=(nb, t // tm),
        in_specs=in_specs, out_specs=out_specs, out_shape=out_shape,
        scratch_shapes=scratch,
        compiler_params=_cparams(2),
        name="inproj",
    )(*args)


def _diff_lambda(lam_ref, lam_init):
    lv = lam_ref[...]
    e1 = jnp.exp(jnp.sum(lv[0:1, :] * lv[1:2, :], axis=-1, keepdims=True))
    e2 = jnp.exp(jnp.sum(lv[2:3, :] * lv[3:4, :], axis=-1, keepdims=True))
    return e1 - e2 + lam_init


def _subln(o, g, lam_init):
    ms = jnp.mean(o * o, axis=-1, keepdims=True)
    return o * lax.rsqrt(ms + EPS) * g * (1.0 - lam_init)


def _softmax_step(s, v_dot, m_ref, l_ref, acc_ref, rows=slice(None)):
    m_prev = m_ref[rows, :]
    m_new = jnp.maximum(m_prev, jnp.max(s, axis=-1, keepdims=True))
    alpha = jnp.exp2(m_prev - m_new)
    p = jnp.exp2(s - m_new)
    l_ref[rows, :] = alpha * l_ref[rows, :] + jnp.sum(p, axis=-1, keepdims=True)
    acc_ref[rows, :] = alpha * acc_ref[rows, :] + v_dot(p.astype(BF16))
    m_ref[rows, :] = m_new


def _attn_kernel(q_ref, k_ref, vt_ref, lam_ref, sg_ref, o_ref, qs_ref, sa_ref, sb_ref, m_ref, acc_ref,
                 *, tq, hps, lam_init):
    i = pl.program_id(2)
    lane = lax.broadcasted_iota(jnp.int32, (tq, DV), 1)
    for hh in range(hps):
        q = q_ref[:, hh * DV:(hh + 1) * DV]
        zero = jnp.zeros_like(q)
        qs_ref[hh, 0:tq, :] = jnp.where(lane < DK, q, zero)
        qs_ref[hh, tq:2 * tq, :] = jnp.where(lane >= DK, q, zero)
    m_ref[...] = jnp.full_like(m_ref, -jnp.inf)
    acc_ref[...] = jnp.zeros_like(acc_ref)

    chains = [(hh, c0, hh * 2 * tq + c0) for hh in range(hps) for c0 in range(0, 2 * tq, ATTN_COLS)]
    ones = jnp.ones((acc_ref.shape[0] - DV, tq), BF16)

    def scores(j, s_ref):
        keys = pl.ds(pl.multiple_of(j * tq, tq), tq)
        for hh, c0, g0 in chains:
            k = k_ref[keys, hh * DV:(hh + 1) * DV]
            s_ref[:, g0:g0 + ATTN_COLS] = _dot_nt(k, qs_ref[hh, c0:c0 + ATTN_COLS, :])

    def step(j, s_ref, diagonal):
        keys = pl.ds(pl.multiple_of(j * tq, tq), tq)
        vts = [jnp.concatenate([vt_ref[hh * DV:(hh + 1) * DV, keys], ones], axis=0)
               for hh in range(hps)]
        for hh, c0, g0 in chains:
            cols = slice(g0, g0 + ATTN_COLS)
            st = s_ref[:, cols]
            if diagonal:
                key = lax.broadcasted_iota(jnp.int32, st.shape, 0)
                qrow = lax.broadcasted_iota(jnp.int32, st.shape, 1) + (c0 % tq)
                st = jnp.where(key <= qrow, st, -jnp.inf)
            m_prev = m_ref[:, cols]
            m_new = jnp.maximum(m_prev, jnp.max(st, axis=0, keepdims=True))
            alpha = jnp.exp2(m_prev - m_new)
            pt = jnp.exp2(st - m_new)
            acc_ref[:, cols] = alpha * acc_ref[:, cols] + _dot(vts[hh], pt.astype(BF16))
            m_ref[:, cols] = m_new

    scores(0, sa_ref)

    def pair(jj, carry):
        j0 = 2 * jj
        scores(j0 + 1, sb_ref)
        step(j0, sa_ref, False)
        scores(j0 + 2, sa_ref)
        step(j0 + 1, sb_ref, False)
        return carry

    lax.fori_loop(0, i // 2, pair, 0)

    @pl.when(i % 2 == 0)
    def _():
        step(i, sa_ref, True)

    @pl.when(i % 2 == 1)
    def _():
        scores(i, sb_ref)
        step(i - 1, sa_ref, False)
        step(i, sb_ref, True)

    lam = _diff_lambda(lam_ref, lam_init)
    for hh in range(hps):
        h0 = hh * 2 * tq
        nt = acc_ref[0:DV, h0:h0 + 2 * tq] / acc_ref[DV:DV + 1, h0:h0 + 2 * tq]
        ot = nt[:, 0:tq] - lam * nt[:, tq:2 * tq]
        ms = jnp.mean(ot * ot, axis=0, keepdims=True)
        ot = ot * lax.rsqrt(ms + EPS)
        o_ref[:, hh * DV:(hh + 1) * DV] = (ot.T * sg_ref[...] * (1.0 - lam_init)).astype(o_ref.dtype)


def _attn(q, kb, vt, lamv, sg, lam_init, tq, hps):
    b, t, w = q.shape
    hw = hps * DV
    return pl.pallas_call(
        functools.partial(_attn_kernel, tq=tq, hps=hps, lam_init=lam_init),
        grid=(b, w // hw, t // tq),
        in_specs=[
            pl.BlockSpec((None, tq, hw), lambda b_, h, i: (b_, i, h)),
            pl.BlockSpec((None, t, hw), lambda b_, h, i: (b_, 0, h)),
            pl.BlockSpec((None, hw, t), lambda b_, h, i: (b_, h, 0)),
            _const_spec(lamv.shape), _const_spec(sg.shape),
        ],
        out_specs=pl.BlockSpec((None, tq, hw), lambda b_, h, i: (b_, i, h)),
        out_shape=jax.ShapeDtypeStruct((b, t, w), BF16),
        scratch_shapes=[
            pltpu.VMEM((hps, 2 * tq, DV), BF16),
            pltpu.VMEM((tq, hps * 2 * tq), F32), pltpu.VMEM((tq, hps * 2 * tq), F32),
            pltpu.VMEM((1, hps * 2 * tq), F32),
            pltpu.VMEM((DV + BF16_ROWS, hps * 2 * tq), F32),
        ],
        compiler_params=_cparams(3),
        name="attn",
    )(q, kb, vt, lamv, sg)


def _sattn_kernel(pt_ref, q_ref, kn_ref, vn_ref, lam_ref, sg_ref, *rest, pp, ts, lam_init):
    k_refs = rest[0:pp]
    v_refs = rest[pp:2 * pp]
    o_ref, qs_ref, m_ref, l_ref, acc_ref = rest[2 * pp:]
    j = pl.program_id(1)
    rows = qs_ref.shape[0]
    hr = 2 * ts
    heads = rows // hr
    page = k_refs[0].shape[0] // heads

    @pl.when(j == 0)
    def _():
        q = q_ref[...]
        lane = lax.broadcasted_iota(jnp.int32, (ts, DV), 1)
        parts = []
        for h in range(heads):
            qh = q[:, h * DV:(h + 1) * DV]
            parts += [jnp.where(lane < DK, qh, 0.0), jnp.where(lane >= DK, qh, 0.0)]
        qs = jnp.concatenate(parts, axis=0).astype(BF16)
        qs_ref[...] = qs
        m_ref[...] = jnp.full_like(m_ref, -jnp.inf)
        l_ref[...] = jnp.zeros_like(l_ref)
        acc_ref[...] = jnp.zeros_like(acc_ref)
        pad = jnp.zeros((DV - ts, DV), F32)
        kns = [jnp.concatenate([kn_ref[:, h * DV:(h + 1) * DV], pad], axis=0).astype(BF16)
               for h in range(heads)]
        vns = [jnp.concatenate([vn_ref[:, h * DV:(h + 1) * DV], pad], axis=0).astype(BF16)
               for h in range(heads)]
        s = jnp.concatenate([_dot_nt(qs[h * hr:(h + 1) * hr, :], kns[h]) for h in range(heads)],
                            axis=0)
        rr = lax.broadcasted_iota(jnp.int32, s.shape, 0)
        cc = lax.broadcasted_iota(jnp.int32, s.shape, 1)
        s = jnp.where(cc <= jnp.bitwise_and(rr, ts - 1), s, -jnp.inf)

        def new_v_dot(pb):
            return jnp.concatenate([_dot(pb[h * hr:(h + 1) * hr, :], vns[h]) for h in range(heads)],
                                   axis=0)

        _softmax_step(s, new_v_dot, m_ref, l_ref, acc_ref)

    def head_rows(refs, h):
        return jnp.concatenate([r[pl.ds(h, page, stride=heads), :] for r in refs],
                               axis=0).astype(BF16)

    hrows = [slice(h * hr, (h + 1) * hr) for h in range(heads)]
    ss = [_dot_nt(qs_ref[hrows[h], :], head_rows(k_refs, h)) for h in range(heads)]
    for h in range(heads):
        _softmax_step(ss[h], lambda pb: _dot(pb, head_rows(v_refs, h)), m_ref, l_ref, acc_ref,
                      hrows[h])

    @pl.when(j == pl.num_programs(1) - 1)
    def _():
        lam = _diff_lambda(lam_ref, lam_init)
        n = acc_ref[...] / l_ref[...]
        outs = []
        for h in range(heads):
            o = n[h * hr:h * hr + ts, :] - lam * n[h * hr + ts:(h + 1) * hr, :]
            outs.append(_subln(o, sg_ref[...], lam_init))
        o_ref[...] = jnp.concatenate(outs, axis=1)


def _sattn(page_table, q, kn, vn, lamv, sg, cache_k, cache_v, layer, lam_init, pp):
    bs, ts, w = q.shape
    n_pages = page_table.shape[1]
    prow = cache_k.shape[2] * cache_k.shape[3]
    ck = cache_k.reshape(cache_k.shape[0], cache_k.shape[1], prow, DV)
    cv = cache_v.reshape(cache_v.shape[0], cache_v.shape[1], prow, DV)
    rows = (w // DK) * ts

    def page_spec(p):
        return pl.BlockSpec((None, None, prow, DV),
                            lambda b, j, pt: (layer, pt[b, j * pp + p], 0, 0))

    new_spec = pl.BlockSpec((None, ts, w), lambda b, j, pt: (b, 0, 0))
    grid_spec = pltpu.PrefetchScalarGridSpec(
        num_scalar_prefetch=1,
        grid=(bs, n_pages // pp),
        in_specs=[new_spec, new_spec, new_spec, _const_spec(lamv.shape), _const_spec(sg.shape)]
        + [page_spec(p) for p in range(pp)] + [page_spec(p) for p in range(pp)],
        out_specs=new_spec,
        scratch_shapes=[
            pltpu.VMEM((rows, DV), BF16),
            pltpu.VMEM((rows, 1), F32), pltpu.VMEM((rows, 1), F32),
            pltpu.VMEM((rows, DV), F32),
        ],
    )
    return pl.pallas_call(
        functools.partial(_sattn_kernel, pp=pp, ts=ts, lam_init=lam_init),
        grid_spec=grid_spec,
        out_shape=jax.ShapeDtypeStruct((bs, ts, w), F32),
        compiler_params=_cparams(2),
        name="sattn",
    )(page_table, q, kn, vn, lamv, sg, *([ck] * pp), *([cv] * pp))


def _post_even_kernel(x_ref, a_ref, o_ref, g_ref, wo_ref, ng_ref, shf_ref, scf_ref, gf_ref,
                      wgu_ref, wd_ref, y_ref):
    d_a = a_ref.shape[1]
    m = _dot(a_ref[...], wo_ref[0:d_a, :]) + _dot(o_ref[...].astype(BF16), wo_ref[d_a:, :])
    x1 = x_ref[...] + g_ref[...] * m
    y_ref[...] = _ffn(x1, ng_ref[...], shf_ref[...], scf_ref[...], gf_ref[...], wgu_ref, wd_ref)


def _post_even(x, a, o, g, w_out, ng, shf, scf, gf, wgu, wd, tm):
    nb, t, d = x.shape
    return pl.pallas_call(
        _post_even_kernel,
        grid=(nb, t // tm),
        in_specs=[
            _rows_spec(tm, d), _rows_spec(tm, a.shape[2]), _rows_spec(tm, o.shape[2]),
            _mod_spec(g, tm, d), _const_spec(w_out.shape), _const_spec(ng.shape),
            _mod_spec(shf, tm, d), _mod_spec(scf, tm, d), _mod_spec(gf, tm, d),
            _layer_spec(wgu), _layer_spec(wd),
        ],
        out_specs=_rows_spec(tm, d),
        out_shape=jax.ShapeDtypeStruct((nb, t, d), F32),
        compiler_params=_cparams(2),
        name="post_even",
    )(x, a, o, g.mods, w_out, ng, shf.mods, scf.mods, gf.mods, wgu.stack, wd.stack)


def _odd_kernel(*refs, tm, short_rows, emit_v):
    (x_ref, sh_ref, sc_ref, ng0_ref, wi_ref, bi_ref, lg_ref, lb_ref, ws_ref, bimg_ref, wo_ref,
     g_ref, ng1_ref, shf_ref, scf_ref, gf_ref, wgu_ref, wd_ref) = refs[:18]
    if emit_v:
        y_ref, v_ref, mix_ref = refs[18:]
    else:
        y_ref, mix_ref = refs[18:]
    d_c = wo_ref.shape[0]
    gw = d_c // G_C

    x = x_ref[...]
    h = _rms_mod(x, ng0_ref[...], sh_ref[...], sc_ref[...]).astype(BF16)
    u = _gelu_tanh(_dot(h, wi_ref[:, 0:d_c]) + bi_ref[:, 0:d_c])
    v = _gelu_tanh(_dot(h, wi_ref[:, d_c:]) + bi_ref[:, d_c:])
    mu = jnp.mean(v, axis=-1, keepdims=True)
    vc = v - mu
    var = jnp.mean(vc * vc, axis=-1, keepdims=True)
    vn = vc * lax.rsqrt(var + EPS) * lg_ref[...] + lb_ref[...]
    if emit_v:
        v_ref[...] = vn
    vb = vn.astype(BF16)

    r = lax.broadcasted_iota(jnp.int32, (CHUNK, CHUNK), 0)
    c = lax.broadcasted_iota(jnp.int32, (CHUNK, CHUNK), 1)
    if short_rows is None:
        for gi in range(G_C):
            wsg = jnp.where(c <= r, ws_ref[gi], 0.0).astype(BF16)
            for n in range(tm // CHUNK):
                blk = _dot(wsg, vb[n * CHUNK:(n + 1) * CHUNK, gi * gw:(gi + 1) * gw])
                mix_ref[n * CHUNK:(n + 1) * CHUNK, gi * gw:(gi + 1) * gw] = (
                    blk + bimg_ref[:, gi * gw:(gi + 1) * gw])
    else:
        rr = lax.broadcasted_iota(jnp.int32, (tm, CHUNK), 0)
        cc = lax.broadcasted_iota(jnp.int32, (tm, CHUNK), 1)
        sel = jnp.where(jnp.bitwise_and(rr, short_rows - 1) == cc, 1.0, 0.0).astype(BF16)
        r2 = lax.broadcasted_iota(jnp.int32, (tm, tm), 0)
        c2 = lax.broadcasted_iota(jnp.int32, (tm, tm), 1)
        same_seq = _shr(r2, short_rows) == _shr(c2, short_rows)
        causal = jnp.bitwise_and(c2, short_rows - 1) <= jnp.bitwise_and(r2, short_rows - 1)
        keep = jnp.where(same_seq, jnp.where(causal, 1.0, 0.0), 0.0)
        for gi in range(G_C):
            spread = _dot_nt(_dot(sel, ws_ref[gi].astype(BF16)).astype(BF16), sel)
            mg = (spread * keep).astype(BF16)
            blk = _dot(mg, vb[:, gi * gw:(gi + 1) * gw])
            mix_ref[:, gi * gw:(gi + 1) * gw] = blk + bimg_ref[:, gi * gw:(gi + 1) * gw]

    y = _dot((u * mix_ref[...]).astype(BF16), wo_ref[...])
    x1 = x + g_ref[...] * y
    y_ref[...] = _ffn(x1, ng1_ref[...], shf_ref[...], scf_ref[...], gf_ref[...], wgu_ref, wd_ref)


def _odd(x, sh, sc, ng0, w_in, b_in, ln_g, ln_b, w_s, bimg, w_out, g, ng1, shf, scf, gf, wgu, wd,
         tm, short_rows, emit_v):
    nb, t, d = x.shape
    d_c = w_out.shape[0]
    out_specs = [_rows_spec(tm, d)]
    out_shape = [jax.ShapeDtypeStruct((nb, t, d), F32)]
    if emit_v:
        out_specs.append(_rows_spec(tm, d_c))
        out_shape.append(jax.ShapeDtypeStruct((nb, t, d_c), F32))
    return pl.pallas_call(
        functools.partial(_odd_kernel, tm=tm, short_rows=short_rows, emit_v=emit_v),
        grid=(nb, t // tm),
        in_specs=[
            _rows_spec(tm, d), _mod_spec(sh, tm, d), _mod_spec(sc, tm, d), _const_spec(ng0.shape),
            _const_spec(w_in.shape), _const_spec(b_in.shape), _const_spec(ln_g.shape),
            _const_spec(ln_b.shape), _const_spec(w_s.shape), _const_spec(bimg.shape),
            _const_spec(w_out.shape), _mod_spec(g, tm, d), _const_spec(ng1.shape),
            _mod_spec(shf, tm, d), _mod_spec(scf, tm, d), _mod_spec(gf, tm, d),
            _layer_spec(wgu), _layer_spec(wd),
        ],
        out_specs=out_specs, out_shape=out_shape,
        scratch_shapes=[pltpu.VMEM((tm, d_c), F32)],
        compiler_params=_cparams(2),
        name="odd",
    )(x, sh.mods, sc.mods, ng0, w_in, b_in, ln_g, ln_b, w_s, bimg, w_out, g.mods, ng1, shf.mods,
      scf.mods, gf.mods, wgu.stack, wd.stack)


def kernel(x_prompt, x_sample, c_prompt, c_sample, cache_k, cache_v, state_conv, page_table, w_ada, b_ada, norm_g, w_in_even, conv_w, q_norm_g, k_norm_g, lam_q1, lam_k1, lam_q2, lam_k2, subln_g, w_out_even, w_in_odd, b_in_odd, ln_g, ln_b, w_s, b_s, w_out_odd, w_gate_up, w_down):
    bp, tp, d = x_prompt.shape
    bs, ts, _ = x_sample.shape
    depth = w_ada.shape[0]
    d_a = conv_w.shape[2]
    qk_w = H_B * 2 * DK
    rows_s = bs * ts

    mods = _adaln(jnp.concatenate([c_prompt, c_sample], axis=0), w_ada, b_ada)

    grp = jnp.arange(MXU_TILE) // DK
    bd = jnp.where(grp[:, None] == grp[None, :], 1.0 / DK, 0.0).astype(BF16)

    wgu_all = w_gate_up.astype(BF16)
    wd_all = w_down.astype(BF16)
    xp = x_prompt
    xs = x_sample.reshape(1, rows_s, d)
    k_p, v_p, conv_p, k_s, v_s, conv_s, chunk_v = [], [], [], [], [], [], []

    for i in range(depth):
        mods_p = mods[i, :bp].reshape(bp, 1, -1)
        mods_s = jnp.repeat(mods[i, bp:], ts, axis=0).reshape(1, rows_s, -1)
        mp = [_ModChunk(mods_p, c) for c in range(6)]
        ms = [_ModChunk(mods_s, c) for c in range(6)]
        ng0 = norm_g[i, 0].reshape(1, d)
        ng1 = norm_g[i, 1].reshape(1, d)
        wgu = _LayerOf(wgu_all, i)
        wd = _LayerOf(wd_all, i)
        if i % 2 == 0:
            e = i // 2
            lam_init = 0.8 - 0.6 * math.exp(-0.3 * i)
            w_in = w_in_even[e].astype(BF16)
            w_out = w_out_even[e].astype(BF16)
            qg = jnp.tile(q_norm_g[e], qk_w // DK).reshape(1, qk_w)
            kg = jnp.tile(k_norm_g[e], qk_w // DK).reshape(1, qk_w)
            lamv = jnp.stack([lam_q1[e], lam_k1[e], lam_q2[e], lam_k2[e]])
            sg = subln_g[e].reshape(1, DV)

            a, q, k, v, kb, vt, tail = _inproj(xp, mp[0], mp[1], ng0, w_in, conv_w[e], qg, kg, bd,
                                               None, ROW_TILE)
            o = _attn(q, kb, vt, lamv, sg, lam_init, ATTN_TILE, ATTN_HEADS)
            xp = _post_even(xp, a, o, mp[2], w_out, ng1, mp[3], mp[4], mp[5], wgu, wd, ROW_TILE)
            k_p.append(k.reshape(bp, tp, H_B, 2 * DK))
            v_p.append(v.reshape(bp, tp, H_B, DV))
            conv_p.append(tail[:, 8 - (CONV_W - 1):, :])

            st = state_conv[e]
            zeros = jnp.zeros((bs, ts - 1, d_a), F32)
            fill1 = jnp.concatenate([st[:, 1:2], zeros], axis=1).reshape(1, rows_s, d_a)
            fill2 = jnp.concatenate([st, zeros[:, 1:]], axis=1).reshape(1, rows_s, d_a)
            a, q, k, v, _, _, xc = _inproj(xs, ms[0], ms[1], ng0, w_in, conv_w[e], qg, kg, bd,
                                           (fill1, fill2), rows_s)
            o = _sattn(page_table, q.astype(F32).reshape(bs, ts, qk_w), k.reshape(bs, ts, qk_w),
                       v.reshape(bs, ts, -1), lamv, sg, cache_k, cache_v, e, lam_init,
                       PAGES_PER_STEP)
            xs = _post_even(xs, a, o.reshape(1, rows_s, -1), ms[2], w_out, ng1, ms[3], ms[4], ms[5],
                            wgu, wd, rows_s)
            k_s.append(k.reshape(bs, ts, H_B, 2 * DK))
            v_s.append(v.reshape(bs, ts, H_B, DV))
            conv_s.append(xc.reshape(bs, ts, d_a)[:, ts - (CONV_W - 1):, :])
        else:
            oi = i // 2
            d_c = w_out_odd.shape[1]
            w_in = w_in_odd[oi].astype(BF16)
            w_out = w_out_odd[oi].astype(BF16)
            b_in = b_in_odd[oi].reshape(1, -1)
            lg = ln_g[oi].reshape(1, d_c)
            lb = ln_b[oi].reshape(1, d_c)
            bimg = jnp.repeat(b_s[oi].T, d_c // G_C, axis=1)
            xp = _odd(xp, mp[0], mp[1], ng0, w_in, b_in, lg, lb, w_s[oi], bimg, w_out, mp[2], ng1,
                      mp[3], mp[4], mp[5], wgu, wd, ROW_TILE, None, False)[0]
            bimg_s = jnp.tile(bimg[:ts], (bs, 1))
            xs, vrows = _odd(xs, ms[0], ms[1], ng0, w_in, b_in, lg, lb, w_s[oi], bimg_s, w_out, ms[2],
                             ng1, ms[3], ms[4], ms[5], wgu, wd, rows_s, ts, True)
            chunk_v.append(vrows.reshape(bs, ts, d_c))

    return (xp, xs.reshape(bs, ts, d), jnp.stack(k_p), jnp.stack(v_p), jnp.stack(conv_p),
            jnp.stack(k_s), jnp.stack(v_s), jnp.stack(conv_s), jnp.stack(chunk_v))
```

```python
import functools
import math
from typing import NamedTuple

import jax
import jax.numpy as jnp
from jax import lax
from jax.experimental import pallas as pl
from jax.experimental.pallas import tpu as pltpu

F32 = jnp.float32
BF16 = jnp.bfloat16
EPS = 1e-6

A_GROUPS = 8
CONV_W = 3
H_B = 4
DK = 64
DV = 2 * DK
CHUNK = 128
G_C = 8
Q_SCALE = DK ** -0.5 * math.log2(math.e)

VMEM_LIMIT_BYTES = 56 * 1024 * 1024
MXU_TILE = 256
BF16_ROWS = 16

ROW_TILE = 512
INPROJ_SUBTILES = 2
ATTN_TILE = 512
ATTN_COLS = 512
ATTN_HEADS = 4
PAGES_PER_STEP = 32


def _cparams(n_axes):
    return pltpu.CompilerParams(
        dimension_semantics=("arbitrary",) * n_axes,
        vmem_limit_bytes=VMEM_LIMIT_BYTES)


def _const_spec(shape):
    zeros = (0,) * len(shape)
    return pl.BlockSpec(shape, lambda *_: zeros, pipeline_mode=pl.Buffered(1))


class _LayerOf(NamedTuple):
    stack: jax.Array
    layer: int


def _layer_spec(w):
    shape = w.stack.shape
    index = (w.layer,) + (0,) * (len(shape) - 1)
    return pl.BlockSpec((None,) + shape[1:], lambda *_: index, pipeline_mode=pl.Buffered(1))


class _ModChunk(NamedTuple):
    mods: jax.Array
    chunk: int


def _mod_spec(m, tm, d):
    chunk = m.chunk
    if m.mods.shape[1] == 1:
        return pl.BlockSpec((None, 1, d), lambda b, i: (b, 0, chunk))
    return pl.BlockSpec((None, tm, d), lambda b, i: (b, i, chunk))


def _rows_spec(tm, width):
    return pl.BlockSpec((None, tm, width), lambda b, i: (b, i, 0))


def _dot(a, b):
    return jnp.dot(a, b, preferred_element_type=F32)


def _dot_nt(a, b):
    return lax.dot_general(a, b, (((1,), (1,)), ((), ())), preferred_element_type=F32)


def _mod_rows(ref, rows):
    return ref[...] if ref.shape[0] == 1 else ref[rows, :]


def _rms_mod(x, g, shift, scale):
    ms = jnp.mean(x * x, axis=-1, keepdims=True)
    y = x * lax.rsqrt(ms + EPS) * g
    return y * (1.0 + scale) + shift


def _shr(x, pow2):
    shift = pow2.bit_length() - 1
    assert 1 << shift == pow2
    return jnp.right_shift(x, shift)


def _silu(x):
    return x / (1.0 + jnp.exp(-x))


def _gelu_tanh(x):
    c = math.sqrt(2.0 / math.pi)
    half = 0.5 * x
    return half + half * jnp.tanh(x * (c + (c * 0.044715) * (x * x)))


def _ffn(x1, ng, shf, scf, gf, wgu_ref, wd_ref):
    d_ff = wd_ref.shape[0]
    n_tiles, rem = divmod(d_ff, MXU_TILE)
    assert rem == 0
    split = (n_tiles + 1) // 2 * MXU_TILE
    h = _rms_mod(x1, ng, shf, scf).astype(BF16)
    acc = None
    for lo, hi in ((0, split), (split, d_ff)):
        g = _dot(h, wgu_ref[:, lo:hi])
        u = _dot(h, wgu_ref[:, d_ff + lo:d_ff + hi])
        a = (_silu(g) * u).astype(BF16)
        d = _dot(a, wd_ref[lo:hi, :])
        acc = d if acc is None else acc + d
    return x1 + gf * acc


def _adaln_kernel(c_ref, w_ref, b_ref, o_ref):
    s = _silu(c_ref[...]).astype(BF16)
    o_ref[...] = _dot(s, w_ref[...].astype(BF16)) + b_ref[...]


def _adaln(c_all, w_ada, b_ada):
    depth, d, d6 = w_ada.shape
    r = c_all.shape[0]
    tn = d6 // 4
    return pl.pallas_call(
        _adaln_kernel,
        grid=(depth, d6 // tn),
        in_specs=[
            pl.BlockSpec((r, d), lambda l, j: (0, 0)),
            pl.BlockSpec((None, d, tn), lambda l, j: (l, 0, j)),
            pl.BlockSpec((None, 1, tn), lambda l, j: (l, 0, j)),
        ],
        out_specs=pl.BlockSpec((None, r, tn), lambda l, j: (l, 0, j)),
        out_shape=jax.ShapeDtypeStruct((depth, r, d6), F32),
        compiler_params=_cparams(2),
        name="adaln",
    )(c_all, w_ada, b_ada.reshape(depth, 1, d6))


def _group_mean_sq(z, bd_ref):
    sq = (z * z).astype(BF16)
    w = bd_ref.shape[0]
    return jnp.concatenate([_dot(sq[:, c:c + w], bd_ref[...]) for c in range(0, z.shape[1], w)],
                           axis=1)


def _inproj_kernel(*refs, tm, n_sub, d_a, qk_w, has_fill, period):
    if has_fill:
        (x_ref, sh_ref, sc_ref, ng_ref, w_ref, cw_ref, qg_ref, kg_ref, bd_ref, f1_ref, f2_ref,
         a_ref, q_ref, k_ref, v_ref, kb_ref, vt_ref, tail_ref) = refs
    else:
        (x_ref, sh_ref, sc_ref, ng_ref, w_ref, cw_ref, qg_ref, kg_ref, bd_ref,
         a_ref, q_ref, k_ref, v_ref, kb_ref, vt_ref, tail_ref, carry_ref) = refs

    if not has_fill:
        @pl.when(pl.program_id(1) == 0)
        def _():
            carry_ref[...] = jnp.zeros_like(carry_ref)

    sub = tm // n_sub
    for r0 in range(0, tm, sub):
        rows = slice(r0, r0 + sub)
        h = _rms_mod(x_ref[rows, :], ng_ref[...], _mod_rows(sh_ref, rows),
                     _mod_rows(sc_ref, rows)).astype(BF16)

        a_b = _dot(h, w_ref[:, 0:d_a])
        xc = _dot(h, w_ref[:, d_a:2 * d_a]) * _dot(h, w_ref[:, 2 * d_a:3 * d_a])
        row = lax.broadcasted_iota(jnp.int32, (sub, d_a), 0)
        if has_fill:
            rp = jnp.bitwise_and(row, period - 1)
            fill1 = f1_ref[rows, :]
            fill2 = f2_ref[rows, :]
        else:
            rp = row
            last1 = carry_ref[pl.ds(7, 1), :]
            last2 = carry_ref[pl.ds(6, 1), :]
            fill1 = jnp.broadcast_to(last1, (sub, d_a))
            fill2 = jnp.where(row == 0, last2, last1)
        prev1 = jnp.where(rp < 1, fill1, pltpu.roll(xc, 1, 0))
        prev2 = jnp.where(rp < 2, fill2, pltpu.roll(xc, 2, 0))
        y = cw_ref[0:1, :] * prev2 + cw_ref[1:2, :] * prev1 + cw_ref[2:3, :] * xc
        a_ref[rows, :] = (a_b * y).astype(BF16)
        if has_fill:
            tail_ref[rows, :] = xc
        else:
            carry_ref[...] = xc[sub - 8:sub, :]
            if r0 + sub == tm:
                tail_ref[...] = xc[sub - 8:sub, :]

        q0 = 3 * d_a
        zq = _dot(h, w_ref[:, q0:q0 + qk_w])
        msq = _group_mean_sq(zq, bd_ref)
        qn = zq * lax.rsqrt(msq + EPS) * qg_ref[...]
        q_ref[rows, :] = (qn * Q_SCALE).astype(BF16)
        zk = _dot(h, w_ref[:, q0 + qk_w:q0 + 2 * qk_w])
        msk = _group_mean_sq(zk, bd_ref)
        kn = zk * lax.rsqrt(msk + EPS) * kg_ref[...]
        kb_ref[rows, :] = kn.astype(BF16)
        zv = _dot(h, w_ref[:, q0 + 2 * qk_w:])
        for out_ref, val in ((k_ref, kn), (v_ref, zv)):
            if has_fill:
                out_ref[rows, :] = val
            else:
                heads = val.shape[1] // DV
                for hh in range(heads):
                    out_ref[pl.ds(r0 * heads + hh, sub, stride=heads), :] = val[:, hh * DV:(hh + 1) * DV]
        vt_ref[:, rows] = zv.T.astype(BF16)


def _inproj(x, sh, sc, ng, w_in, cw, qg, kg, bd, fills, tm):
    nb, t, d = x.shape
    d_a = cw.shape[1]
    qk_w = qg.shape[1]
    d_b = w_in.shape[1] - 3 * d_a - 2 * qk_w
    has_fill = fills is not None
    in_specs = [
        _rows_spec(tm, d), _mod_spec(sh, tm, d), _mod_spec(sc, tm, d),
        _const_spec(ng.shape), _const_spec(w_in.shape), _const_spec(cw.shape),
        _const_spec(qg.shape), _const_spec(kg.shape), _const_spec(bd.shape),
    ]
    args = [x, sh.mods, sc.mods, ng, w_in, cw, qg, kg, bd]
    scratch = []
    if has_fill:
        in_specs += [_rows_spec(tm, d_a), _rows_spec(tm, d_a)]
        args += list(fills)
        tail_spec = _rows_spec(tm, d_a)
        tail_shape = (nb, t, d_a)
    else:
        scratch = [pltpu.VMEM((8, d_a), F32)]
        tail_spec = pl.BlockSpec((None, 8, d_a), lambda b, i: (b, 0, 0))
        tail_shape = (nb, 8, d_a)
    if has_fill:
        kv_specs = [_rows_spec(tm, qk_w), _rows_spec(tm, d_b)]
        kv_shapes = [(nb, t, qk_w), (nb, t, d_b)]
    else:
        kv_specs = [_rows_spec(tm * (qk_w // DV), DV), _rows_spec(tm * (d_b // DV), DV)]
        kv_shapes = [(nb, t * (qk_w // DV), DV), (nb, t * (d_b // DV), DV)]
    out_specs = [_rows_spec(tm, d_a), _rows_spec(tm, qk_w), *kv_specs,
                 _rows_spec(tm, qk_w), pl.BlockSpec((None, d_b, tm), lambda b, i: (b, 0, i)), tail_spec]
    out_shape = [
        jax.ShapeDtypeStruct((nb, t, d_a), BF16), jax.ShapeDtypeStruct((nb, t, qk_w), BF16),
        jax.ShapeDtypeStruct(kv_shapes[0], F32), jax.ShapeDtypeStruct(kv_shapes[1], F32),
        jax.ShapeDtypeStruct((nb, t, qk_w), BF16), jax.ShapeDtypeStruct((nb, d_b, t), BF16),
        jax.ShapeDtypeStruct(tail_shape, F32),
    ]
    return pl.pallas_call(
        functools.partial(_inproj_kernel, tm=tm, n_sub=1 if has_fill else INPROJ_SUBTILES, d_a=d_a,
                          qk_w=qk_w, has_fill=has_fill, period=8),
        grid=(nb, t // tm),
        in_specs=in_specs, out_specs=out_specs, out_shape=out_shape,
        scratch_shapes=scratch,
        compiler_params=_cparams(2),
        name="inproj",
    )(*args)


def _diff_lambda(lam_ref, lam_init):
    lv = lam_ref[...]
    e1 = jnp.exp(jnp.sum(lv[0:1, :] * lv[1:2, :], axis=-1, keepdims=True))
    e2 = jnp.exp(jnp.sum(lv[2:3, :] * lv[3:4, :], axis=-1, keepdims=True))
    return e1 - e2 + lam_init


def _subln(o, g, lam_init):
    ms = jnp.mean(o * o, axis=-1, keepdims=True)
    return o * lax.rsqrt(ms + EPS) * g * (1.0 - lam_init)


def _softmax_step(s, v_dot, m_ref, l_ref, acc_ref, rows=slice(None)):
    m_prev = m_ref[rows, :]
    m_new = jnp.maximum(m_prev, jnp.max(s, axis=-1, keepdims=True))
    alpha = jnp.exp2(m_prev - m_new)
    p = jnp.exp2(s - m_new)
    l_ref[rows, :] = alpha * l_ref[rows, :] + jnp.sum(p, axis=-1, keepdims=True)
    acc_ref[rows, :] = alpha * acc_ref[rows, :] + v_dot(p.astype(BF16))
    m_ref[rows, :] = m_new


def _attn_kernel(q_ref, k_ref, vt_ref, lam_ref, sg_ref, o_ref, qs_ref, sa_ref, sb_ref, m_ref, acc_ref,
                 *, tq, hps, lam_init):
    i = pl.program_id(2)
    lane = lax.broadcasted_iota(jnp.int32, (tq, DV), 1)
    for hh in range(hps):
        q = q_ref[:, hh * DV:(hh + 1) * DV]
        zero = jnp.zeros_like(q)
        qs_ref[hh, 0:tq, :] = jnp.where(lane < DK, q, zero)
        qs_ref[hh, tq:2 * tq, :] = jnp.where(lane >= DK, q, zero)
    m_ref[...] = jnp.full_like(m_ref, -jnp.inf)
    acc_ref[...] = jnp.zeros_like(acc_ref)

    chains = [(hh, c0, hh * 2 * tq + c0) for hh in range(hps) for c0 in range(0, 2 * tq, ATTN_COLS)]
    ones = jnp.ones((acc_ref.shape[0] - DV, tq), BF16)

    def scores(j, s_ref):
        keys = pl.ds(pl.multiple_of(j * tq, tq), tq)
        for hh, c0, g0 in chains:
            k = k_ref[keys, hh * DV:(hh + 1) * DV]
            s_ref[:, g0:g0 + ATTN_COLS] = _dot_nt(k, qs_ref[hh, c0:c0 + ATTN_COLS, :])

    def step(j, s_ref, diagonal):
        keys = pl.ds(pl.multiple_of(j * tq, tq), tq)
        vts = [jnp.concatenate([vt_ref[hh * DV:(hh + 1) * DV, keys], ones], axis=0)
               for hh in range(hps)]
        for hh, c0, g0 in chains:
            cols = slice(g0, g0 + ATTN_COLS)
            st = s_ref[:, cols]
            if diagonal:
                key = lax.broadcasted_iota(jnp.int32, st.shape, 0)
                qrow = lax.broadcasted_iota(jnp.int32, st.shape, 1) + (c0 % tq)
                st = jnp.where(key <= qrow, st, -jnp.inf)
            m_prev = m_ref[:, cols]
            m_new = jnp.maximum(m_prev, jnp.max(st, axis=0, keepdims=True))
            alpha = jnp.exp2(m_prev - m_new)
            pt = jnp.exp2(st - m_new)
            acc_ref[:, cols] = alpha * acc_ref[:, cols] + _dot(vts[hh], pt.astype(BF16))
            m_ref[:, cols] = m_new

    scores(0, sa_ref)

    def pair(jj, carry):
        j0 = 2 * jj
        scores(j0 + 1, sb_ref)
        step(j0, sa_ref, False)
        scores(j0 + 2, sa_ref)
        step(j0 + 1, sb_ref, False)
        return carry

    lax.fori_loop(0, i // 2, pair, 0)

    @pl.when(i % 2 == 0)
    def _():
        step(i, sa_ref, True)

    @pl.when(i % 2 == 1)
    def _():
        scores(i, sb_ref)
        step(i - 1, sa_ref, False)
        step(i, sb_ref, True)

    lam = _diff_lambda(lam_ref, lam_init)
    for hh in range(hps):
        h0 = hh * 2 * tq
        nt = acc_ref[0:DV, h0:h0 + 2 * tq] / acc_ref[DV:DV + 1, h0:h0 + 2 * tq]
        ot = nt[:, 0:tq] - lam * nt[:, tq:2 * tq]
        ms = jnp.mean(ot * ot, axis=0, keepdims=True)
        ot = ot * lax.rsqrt(ms + EPS)
        o_ref[:, hh * DV:(hh + 1) * DV] = (ot.T * sg_ref[...] * (1.0 - lam_init)).astype(o_ref.dtype)


def _attn(q, kb, vt, lamv, sg, lam_init, tq, hps):
    b, t, w = q.shape
    hw = hps * DV
    return pl.pallas_call(
        functools.partial(_attn_kernel, tq=tq, hps=hps, lam_init=lam_init),
        grid=(b, w // hw, t // tq),
        in_specs=[
            pl.BlockSpec((None, tq, hw), lambda b_, h, i: (b_, i, h)),
            pl.BlockSpec((None, t, hw), lambda b_, h, i: (b_, 0, h)),
            pl.BlockSpec((None, hw, t), lambda b_, h, i: (b_, h, 0)),
            _const_spec(lamv.shape), _const_spec(sg.shape),
        ],
        out_specs=pl.BlockSpec((None, tq, hw), lambda b_, h, i: (b_, i, h)),
        out_shape=jax.ShapeDtypeStruct((b, t, w), BF16),
        scratch_shapes=[
            pltpu.VMEM((hps, 2 * tq, DV), BF16),
            pltpu.VMEM((tq, hps * 2 * tq), F32), pltpu.VMEM((tq, hps * 2 * tq), F32),
            pltpu.VMEM((1, hps * 2 * tq), F32),
            pltpu.VMEM((DV + BF16_ROWS, hps * 2 * tq), F32),
        ],
        compiler_params=_cparams(3),
        name="attn",
    )(q, kb, vt, lamv, sg)


def _sattn_kernel(pt_ref, q_ref, kn_ref, vn_ref, lam_ref, sg_ref, *rest, pp, ts, lam_init):
    k_refs = rest[0:pp]
    v_refs = rest[pp:2 * pp]
    o_ref, qs_ref, m_ref, l_ref, acc_ref = rest[2 * pp:]
    j = pl.program_id(1)
    rows = qs_ref.shape[0]
    hr = 2 * ts
    heads = rows // hr
    page = k_refs[0].shape[0] // heads

    @pl.when(j == 0)
    def _():
        q = q_ref[...]
        lane = lax.broadcasted_iota(jnp.int32, (ts, DV), 1)
        parts = []
        for h in range(heads):
            qh = q[:, h * DV:(h + 1) * DV]
            parts += [jnp.where(lane < DK, qh, 0.0), jnp.where(lane >= DK, qh, 0.0)]
        qs = jnp.concatenate(parts, axis=0).astype(BF16)
        qs_ref[...] = qs
        m_ref[...] = jnp.full_like(m_ref, -jnp.inf)
        l_ref[...] = jnp.zeros_like(l_ref)
        acc_ref[...] = jnp.zeros_like(acc_ref)
        pad = jnp.zeros((DV - ts, DV), F32)
        kns = [jnp.concatenate([kn_ref[:, h * DV:(h + 1) * DV], pad], axis=0).astype(BF16)
               for h in range(heads)]
        vns = [jnp.concatenate([vn_ref[:, h * DV:(h + 1) * DV], pad], axis=0).astype(BF16)
               for h in range(heads)]
        s = jnp.concatenate([_dot_nt(qs[h * hr:(h + 1) * hr, :], kns[h]) for h in range(heads)],
                            axis=0)
        rr = lax.broadcasted_iota(jnp.int32, s.shape, 0)
        cc = lax.broadcasted_iota(jnp.int32, s.shape, 1)
        s = jnp.where(cc <= jnp.bitwise_and(rr, ts - 1), s, -jnp.inf)

        def new_v_dot(pb):
            return jnp.concatenate([_dot(pb[h * hr:(h + 1) * hr, :], vns[h]) for h in range(heads)],
                                   axis=0)

        _softmax_step(s, new_v_dot, m_ref, l_ref, acc_ref)

    def head_rows(refs, h):
        return jnp.concatenate([r[pl.ds(h, page, stride=heads), :] for r in refs],
                               axis=0).astype(BF16)

    hrows = [slice(h * hr, (h + 1) * hr) for h in range(heads)]
    ss = [_dot_nt(qs_ref[hrows[h], :], head_rows(k_refs, h)) for h in range(heads)]
    for h in range(heads):
        _softmax_step(ss[h], lambda pb: _dot(pb, head_rows(v_refs, h)), m_ref, l_ref, acc_ref,
                      hrows[h])

    @pl.when(j == pl.num_programs(1) - 1)
    def _():
        lam = _diff_lambda(lam_ref, lam_init)
        n = acc_ref[...] / l_ref[...]
        outs = []
        for h in range(heads):
            o = n[h * hr:h * hr + ts, :] - lam * n[h * hr + ts:(h + 1) * hr, :]
            outs.append(_subln(o, sg_ref[...], lam_init))
        o_ref[...] = jnp.concatenate(outs, axis=1)


def _sattn(page_table, q, kn, vn, lamv, sg, cache_k, cache_v, layer, lam_init, pp):
    bs, ts, w = q.shape
    n_pages = page_table.shape[1]
    prow = cache_k.shape[2] * cache_k.shape[3]
    ck = cache_k.reshape(cache_k.shape[0], cache_k.shape[1], prow, DV)
    cv = cache_v.reshape(cache_v.shape[0], cache_v.shape[1], prow, DV)
    rows = (w // DK) * ts

    def page_spec(p):
        return pl.BlockSpec((None, None, prow, DV),
                            lambda b, j, pt: (layer, pt[b, j * pp + p], 0, 0))

    new_spec = pl.BlockSpec((None, ts, w), lambda b, j, pt: (b, 0, 0))
    grid_spec = pltpu.PrefetchScalarGridSpec(
        num_scalar_prefetch=1,
        grid=(bs, n_pages // pp),
        in_specs=[new_spec, new_spec, new_spec, _const_spec(lamv.shape), _const_spec(sg.shape)]
        + [page_spec(p) for p in range(pp)] + [page_spec(p) for p in range(pp)],
        out_specs=new_spec,
        scratch_shapes=[
            pltpu.VMEM((rows, DV), BF16),
            pltpu.VMEM((rows, 1), F32), pltpu.VMEM((rows, 1), F32),
            pltpu.VMEM((rows, DV), F32),
        ],
    )
    return pl.pallas_call(
        functools.partial(_sattn_kernel, pp=pp, ts=ts, lam_init=lam_init),
        grid_spec=grid_spec,
        out_shape=jax.ShapeDtypeStruct((bs, ts, w), F32),
        compiler_params=_cparams(2),
        name="sattn",
    )(page_table, q, kn, vn, lamv, sg, *([ck] * pp), *([cv] * pp))


def _post_even_kernel(x_ref, a_ref, o_ref, g_ref, wo_ref, ng_ref, shf_ref, scf_ref, gf_ref,
                      wgu_ref, wd_ref, y_ref):
    d_a = a_ref.shape[1]
    m = _dot(a_ref[...], wo_ref[0:d_a, :]) + _dot(o_ref[...].astype(BF16), wo_ref[d_a:, :])
    x1 = x_ref[...] + g_ref[...] * m
    y_ref[...] = _ffn(x1, ng_ref[...], shf_ref[...], scf_ref[...], gf_ref[...], wgu_ref, wd_ref)


def _post_even(x, a, o, g, w_out, ng, shf, scf, gf, wgu, wd, tm):
    nb, t, d = x.shape
    return pl.pallas_call(
        _post_even_kernel,
        grid=(nb, t // tm),
        in_specs=[
            _rows_spec(tm, d), _rows_spec(tm, a.shape[2]), _rows_spec(tm, o.shape[2]),
            _mod_spec(g, tm, d), _const_spec(w_out.shape), _const_spec(ng.shape),
            _mod_spec(shf, tm, d), _mod_spec(scf, tm, d), _mod_spec(gf, tm, d),
            _layer_spec(wgu), _layer_spec(wd),
        ],
        out_specs=_rows_spec(tm, d),
        out_shape=jax.ShapeDtypeStruct((nb, t, d), F32),
        compiler_params=_cparams(2),
        name="post_even",
    )(x, a, o, g.mods, w_out, ng, shf.mods, scf.mods, gf.mods, wgu.stack, wd.stack)


def _odd_kernel(*refs, tm, short_rows, emit_v):
    (x_ref, sh_ref, sc_ref, ng0_ref, wi_ref, bi_ref, lg_ref, lb_ref, ws_ref, bimg_ref, wo_ref,
     g_ref, ng1_ref, shf_ref, scf_ref, gf_ref, wgu_ref, wd_ref) = refs[:18]
    if emit_v:
        y_ref, v_ref, mix_ref = refs[18:]
    else:
        y_ref, mix_ref = refs[18:]
    d_c = wo_ref.shape[0]
    gw = d_c // G_C

    x = x_ref[...]
    h = _rms_mod(x, ng0_ref[...], sh_ref[...], sc_ref[...]).astype(BF16)
    u = _gelu_tanh(_dot(h, wi_ref[:, 0:d_c]) + bi_ref[:, 0:d_c])
    v = _gelu_tanh(_dot(h, wi_ref[:, d_c:]) + bi_ref[:, d_c:])
    mu = jnp.mean(v, axis=-1, keepdims=True)
    vc = v - mu
    var = jnp.mean(vc * vc, axis=-1, keepdims=True)
    vn = vc * lax.rsqrt(var + EPS) * lg_ref[...] + lb_ref[...]
    if emit_v:
        v_ref[...] = vn
    vb = vn.astype(BF16)

    r = lax.broadcasted_iota(jnp.int32, (CHUNK, CHUNK), 0)
    c = lax.broadcasted_iota(jnp.int32, (CHUNK, CHUNK), 1)
    if short_rows is None:
        for gi in range(G_C):
            wsg = jnp.where(c <= r, ws_ref[gi], 0.0).astype(BF16)
            for n in range(tm // CHUNK):
                blk = _dot(wsg, vb[n * CHUNK:(n + 1) * CHUNK, gi * gw:(gi + 1) * gw])
                mix_ref[n * CHUNK:(n + 1) * CHUNK, gi * gw:(gi + 1) * gw] = (
                    blk + bimg_ref[:, gi * gw:(gi + 1) * gw])
    else:
        rr = lax.broadcasted_iota(jnp.int32, (tm, CHUNK), 0)
        cc = lax.broadcasted_iota(jnp.int32, (tm, CHUNK), 1)
        sel = jnp.where(jnp.bitwise_and(rr, short_rows - 1) == cc, 1.0, 0.0).astype(BF16)
        r2 = lax.broadcasted_iota(jnp.int32, (tm, tm), 0)
        c2 = lax.broadcasted_iota(jnp.int32, (tm, tm), 1)
        same_seq = _shr(r2, short_rows) == _shr(c2, short_rows)
        causal = jnp.bitwise_and(c2, short_rows - 1) <= jnp.bitwise_and(r2, short_rows - 1)
        keep = jnp.where(same_seq, jnp.where(causal, 1.0, 0.0), 0.0)
        for gi in range(G_C):
            spread = _dot_nt(_dot(sel, ws_ref[gi].astype(BF16)).astype(BF16), sel)
            mg = (spread * keep).astype(BF16)
            blk = _dot(mg, vb[:, gi * gw:(gi + 1) * gw])
            mix_ref[:, gi * gw:(gi + 1) * gw] = blk + bimg_ref[:, gi * gw:(gi + 1) * gw]

    y = _dot((u * mix_ref[...]).astype(BF16), wo_ref[...])
    x1 = x + g_ref[...] * y
    y_ref[...] = _ffn(x1, ng1_ref[...], shf_ref[...], scf_ref[...], gf_ref[...], wgu_ref, wd_ref)


def _odd(x, sh, sc, ng0, w_in, b_in, ln_g, ln_b, w_s, bimg, w_out, g, ng1, shf, scf, gf, wgu, wd,
         tm, short_rows, emit_v):
    nb, t, d = x.shape
    d_c = w_out.shape[0]
    out_specs = [_rows_spec(tm, d)]
    out_shape = [jax.ShapeDtypeStruct((nb, t, d), F32)]
    if emit_v:
        out_specs.append(_rows_spec(tm, d_c))
        out_shape.append(jax.ShapeDtypeStruct((nb, t, d_c), F32))
    return pl.pallas_call(
        functools.partial(_odd_kernel, tm=tm, short_rows=short_rows, emit_v=emit_v),
        grid=(nb, t // tm),
        in_specs=[
            _rows_spec(tm, d), _mod_spec(sh, tm, d), _mod_spec(sc, tm, d), _const_spec(ng0.shape),
            _const_spec(w_in.shape), _const_spec(b_in.shape), _const_spec(ln_g.shape),
            _const_spec(ln_b.shape), _const_spec(w_s.shape), _const_spec(bimg.shape),
            _const_spec(w_out.shape), _mod_spec(g, tm, d), _const_spec(ng1.shape),
            _mod_spec(shf, tm, d), _mod_spec(scf, tm, d), _mod_spec(gf, tm, d),
            _layer_spec(wgu), _layer_spec(wd),
        ],
        out_specs=out_specs, out_shape=out_shape,
        scratch_shapes=[pltpu.VMEM((tm, d_c), F32)],
        compiler_params=_cparams(2),
        name="odd",
    )(x, sh.mods, sc.mods, ng0, w_in, b_in, ln_g, ln_b, w_s, bimg, w_out, g.mods, ng1, shf.mods,
      scf.mods, gf.mods, wgu.stack, wd.stack)


def kernel(x_prompt, x_sample, c_prompt, c_sample, cache_k, cache_v, state_conv, page_table, w_ada, b_ada, norm_g, w_in_even, conv_w, q_norm_g, k_norm_g, lam_q1, lam_k1, lam_q2, lam_k2, subln_g, w_out_even, w_in_odd, b_in_odd, ln_g, ln_b, w_s, b_s, w_out_odd, w_gate_up, w_down):
    bp, tp, d = x_prompt.shape
    bs, ts, _ = x_sample.shape
    depth = w_ada.shape[0]
    d_a = conv_w.shape[2]
    qk_w = H_B * 2 * DK
    rows_s = bs * ts

    mods = _adaln(jnp.concatenate([c_prompt, c_sample], axis=0), w_ada, b_ada)

    grp = jnp.arange(MXU_TILE) // DK
    bd = jnp.where(grp[:, None] == grp[None, :], 1.0 / DK, 0.0).astype(BF16)

    wgu_all = w_gate_up.astype(BF16)
    wd_all = w_down.astype(BF16)
    xp = x_prompt
    xs = x_sample.reshape(1, rows_s, d)
    k_p, v_p, conv_p, k_s, v_s, conv_s, chunk_v = [], [], [], [], [], [], []

    for i in range(depth):
        mods_p = mods[i, :bp].reshape(bp, 1, -1)
        mods_s = jnp.repeat(mods[i, bp:], ts, axis=0).reshape(1, rows_s, -1)
        mp = [_ModChunk(mods_p, c) for c in range(6)]
        ms = [_ModChunk(mods_s, c) for c in range(6)]
        ng0 = norm_g[i, 0].reshape(1, d)
        ng1 = norm_g[i, 1].reshape(1, d)
        wgu = _LayerOf(wgu_all, i)
        wd = _LayerOf(wd_all, i)
        if i % 2 == 0:
            e = i // 2
            lam_init = 0.8 - 0.6 * math.exp(-0.3 * i)
            w_in = w_in_even[e].astype(BF16)
            w_out = w_out_even[e].astype(BF16)
            qg = jnp.tile(q_norm_g[e], qk_w // DK).reshape(1, qk_w)
            kg = jnp.tile(k_norm_g[e], qk_w // DK).reshape(1, qk_w)
            lamv = jnp.stack([lam_q1[e], lam_k1[e], lam_q2[e], lam_k2[e]])
            sg = subln_g[e].reshape(1, DV)

            a, q, k, v, kb, vt, tail = _inproj(xp, mp[0], mp[1], ng0, w_in, conv_w[e], qg, kg, bd,
                                               None, ROW_TILE)
            o = _attn(q, kb, vt, lamv, sg, lam_init, ATTN_TILE, ATTN_HEADS)
            xp = _post_even(xp, a, o, mp[2], w_out, ng1, mp[3], mp[4], mp[5], wgu, wd, ROW_TILE)
            k_p.append(k.reshape(bp, tp, H_B, 2 * DK))
            v_p.append(v.reshape(bp, tp, H_B, DV))
            conv_p.append(tail[:, 8 - (CONV_W - 1):, :])

            st = state_conv[e]
            zeros = jnp.zeros((bs, ts - 1, d_a), F32)
            fill1 = jnp.concatenate([st[:, 1:2], zeros], axis=1).reshape(1, rows_s, d_a)
            fill2 = jnp.concatenate([st, zeros[:, 1:]], axis=1).reshape(1, rows_s, d_a)
            a, q, k, v, _, _, xc = _inproj(xs, ms[0], ms[1], ng0, w_in, conv_w[e], qg, kg, bd,
                                           (fill1, fill2), rows_s)
            o = _sattn(page_table, q.astype(F32).reshape(bs, ts, qk_w), k.reshape(bs, ts, qk_w),
                       v.reshape(bs, ts, -1), lamv, sg, cache_k, cache_v, e, lam_init,
                       PAGES_PER_STEP)
            xs = _post_even(xs, a, o.reshape(1, rows_s, -1), ms[2], w_out, ng1, ms[3], ms[4], ms[5],
                            wgu, wd, rows_s)
            k_s.append(k.reshape(bs, ts, H_B, 2 * DK))
            v_s.append(v.reshape(bs, ts, H_B, DV))
            conv_s.append(xc.reshape(bs, ts, d_a)[:, ts - (CONV_W - 1):, :])
        else:
            oi = i // 2
            d_c = w_out_odd.shape[1]
            w_in = w_in_odd[oi].astype(BF16)
            w_out = w_out_odd[oi].astype(BF16)
            b_in = b_in_odd[oi].reshape(1, -1)
            lg = ln_g[oi].reshape(1, d_c)
            lb = ln_b[oi].reshape(1, d_c)
            bimg = jnp.repeat(b_s[oi].T, d_c // G_C, axis=1)
            xp = _odd(xp, mp[0], mp[1], ng0, w_in, b_in, lg, lb, w_s[oi], bimg, w_out, mp[2], ng1,
                      mp[3], mp[4], mp[5], wgu, wd, ROW_TILE, None, False)[0]
            bimg_s = jnp.tile(bimg[:ts], (bs, 1))
            xs, vrows = _odd(xs, ms[0], ms[1], ng0, w_in, b_in, lg, lb, w_s[oi], bimg_s, w_out, ms[2],
                             ng1, ms[3], ms[4], ms[5], wgu, wd, rows_s, ts, True)
            chunk_v.append(vrows.reshape(bs, ts, d_c))

    return (xp, xs.reshape(bs, ts, d), jnp.stack(k_p), jnp.stack(v_p), jnp.stack(conv_p),
            jnp.stack(k_s), jnp.stack(v_s), jnp.stack(conv_s), jnp.stack(chunk_v))
```

```python
import functools
import math
from typing import NamedTuple

import jax
import jax.numpy as jnp
from jax import lax
from jax.experimental import pallas as pl
from jax.experimental.pallas import tpu as pltpu

F32 = jnp.float32
BF16 = jnp.bfloat16
EPS = 1e-6

A_GROUPS = 8
CONV_W = 3
H_B = 4
DK = 64
DV = 2 * DK
CHUNK = 128
G_C = 8
Q_SCALE = DK ** -0.5 * math.log2(math.e)

VMEM_LIMIT_BYTES = 56 * 1024 * 1024
MXU_TILE = 256
BF16_ROWS = 16

ROW_TILE = 512
INPROJ_TILE = 1024
INPROJ_SUBTILES = 4
ATTN_TILE = 512
ATTN_COLS = 512
ATTN_HEADS = 4
PAGES_PER_STEP = 32


def _cparams(n_axes):
    return pltpu.CompilerParams(
        dimension_semantics=("arbitrary",) * n_axes,
        vmem_limit_bytes=VMEM_LIMIT_BYTES)


def _const_spec(shape):
    zeros = (0,) * len(shape)
    return pl.BlockSpec(shape, lambda *_: zeros, pipeline_mode=pl.Buffered(1))


class _LayerOf(NamedTuple):
    stack: jax.Array
    layer: int


def _layer_spec(w):
    shape = w.stack.shape
    index = (w.layer,) + (0,) * (len(shape) - 1)
    return pl.BlockSpec((None,) + shape[1:], lambda *_: index, pipeline_mode=pl.Buffered(1))


class _ModChunk(NamedTuple):
    mods: jax.Array
    chunk: int


def _mod_spec(m, tm, d):
    chunk = m.chunk
    if m.mods.shape[1] == 1:
        return pl.BlockSpec((None, 1, d), lambda b, i: (b, 0, chunk))
    return pl.BlockSpec((None, tm, d), lambda b, i: (b, i, chunk))


def _rows_spec(tm, width):
    return pl.BlockSpec((None, tm, width), lambda b, i: (b, i, 0))


def _dot(a, b):
    return jnp.dot(a, b, preferred_element_type=F32)


def _dot_nt(a, b):
    return lax.dot_general(a, b, (((1,), (1,)), ((), ())), preferred_element_type=F32)


def _mod_rows(ref, rows):
    return ref[...] if ref.shape[0] == 1 else ref[rows, :]


def _rms_mod(x, g, shift, scale):
    ms = jnp.mean(x * x, axis=-1, keepdims=True)
    y = x * lax.rsqrt(ms + EPS) * g
    return y * (1.0 + scale) + shift


def _shr(x, pow2):
    shift = pow2.bit_length() - 1
    assert 1 << shift == pow2
    return jnp.right_shift(x, shift)


def _silu(x):
    return x / (1.0 + jnp.exp(-x))


def _gelu_tanh(x):
    c = math.sqrt(2.0 / math.pi)
    half = 0.5 * x
    return half + half * jnp.tanh(x * (c + (c * 0.044715) * (x * x)))


def _ffn(x1, ng, shf, scf, gf, wgu_ref, wd_ref):
    d_ff = wd_ref.shape[0]
    n_tiles, rem = divmod(d_ff, MXU_TILE)
    assert rem == 0
    split = (n_tiles + 1) // 2 * MXU_TILE
    h = _rms_mod(x1, ng, shf, scf).astype(BF16)
    acc = None
    for lo, hi in ((0, split), (split, d_ff)):
        g = _dot(h, wgu_ref[:, lo:hi])
        u = _dot(h, wgu_ref[:, d_ff + lo:d_ff + hi])
        a = (_silu(g) * u).astype(BF16)
        d = _dot(a, wd_ref[lo:hi, :])
        acc = d if acc is None else acc + d
    return x1 + gf * acc


def _adaln_kernel(c_ref, w_ref, b_ref, o_ref):
    s = _silu(c_ref[...])
    o_ref[...] = jnp.dot(s, w_ref[...], precision=lax.Precision.HIGHEST,
                         preferred_element_type=F32) + b_ref[...]


def _adaln(c_all, w_ada, b_ada):
    depth, d, d6 = w_ada.shape
    r = c_all.shape[0]
    tn = d6 // 4
    return pl.pallas_call(
        _adaln_kernel,
        grid=(depth, d6 // tn),
        in_specs=[
            pl.BlockSpec((r, d), lambda l, j: (0, 0)),
            pl.BlockSpec((None, d, tn), lambda l, j: (l, 0, j)),
            pl.BlockSpec((None, 1, tn), lambda l, j: (l, 0, j)),
        ],
        out_specs=pl.BlockSpec((None, r, tn), lambda l, j: (l, 0, j)),
        out_shape=jax.ShapeDtypeStruct((depth, r, d6), F32),
        compiler_params=_cparams(2),
        name="adaln",
    )(c_all, w_ada, b_ada.reshape(depth, 1, d6))


def _group_mean_sq(z, bd_ref):
    sq = (z * z).astype(BF16)
    w = bd_ref.shape[0]
    return jnp.concatenate([_dot(sq[:, c:c + w], bd_ref[...]) for c in range(0, z.shape[1], w)],
                           axis=1)


def _inproj_kernel(*refs, tm, n_sub, d_a, qk_w, has_fill, period):
    if has_fill:
        (x_ref, sh_ref, sc_ref, ng_ref, w_ref, cw_ref, qg_ref, kg_ref, bd_ref, f1_ref, f2_ref,
         a_ref, q_ref, k_ref, v_ref, kb_ref, vt_ref, tail_ref) = refs
    else:
        (x_ref, sh_ref, sc_ref, ng_ref, w_ref, cw_ref, qg_ref, kg_ref, bd_ref,
         a_ref, q_ref, k_ref, v_ref, kb_ref, vt_ref, tail_ref, carry_ref) = refs

    if not has_fill:
        @pl.when(pl.program_id(1) == 0)
        def _():
            carry_ref[...] = jnp.zeros_like(carry_ref)

    sub = tm // n_sub
    for r0 in range(0, tm, sub):
        rows = slice(r0, r0 + sub)
        h = _rms_mod(x_ref[rows, :], ng_ref[...], _mod_rows(sh_ref, rows),
                     _mod_rows(sc_ref, rows)).astype(BF16)

        a_b = _dot(h, w_ref[:, 0:d_a])
        xc = _dot(h, w_ref[:, d_a:2 * d_a]) * _dot(h, w_ref[:, 2 * d_a:3 * d_a])
        row = lax.broadcasted_iota(jnp.int32, (sub, d_a), 0)
        if has_fill:
            rp = jnp.bitwise_and(row, period - 1)
            fill1 = f1_ref[rows, :]
            fill2 = f2_ref[rows, :]
        else:
            rp = row
            last1 = carry_ref[pl.ds(7, 1), :]
            last2 = carry_ref[pl.ds(6, 1), :]
            fill1 = jnp.broadcast_to(last1, (sub, d_a))
            fill2 = jnp.where(row == 0, last2, last1)
        prev1 = jnp.where(rp < 1, fill1, pltpu.roll(xc, 1, 0))
        prev2 = jnp.where(rp < 2, fill2, pltpu.roll(xc, 2, 0))
        y = cw_ref[0:1, :] * prev2 + cw_ref[1:2, :] * prev1 + cw_ref[2:3, :] * xc
        a_ref[rows, :] = (a_b * y).astype(BF16)
        if has_fill:
            tail_ref[rows, :] = xc
        else:
            carry_ref[...] = xc[sub - 8:sub, :]
            if r0 + sub == tm:
                tail_ref[...] = xc[sub - 8:sub, :]

        q0 = 3 * d_a
        zq = _dot(h, w_ref[:, q0:q0 + qk_w])
        msq = _group_mean_sq(zq, bd_ref)
        qn = zq * lax.rsqrt(msq + EPS) * qg_ref[...]
        q_ref[rows, :] = (qn * Q_SCALE).astype(BF16)
        zk = _dot(h, w_ref[:, q0 + qk_w:q0 + 2 * qk_w])
        msk = _group_mean_sq(zk, bd_ref)
        kn = zk * lax.rsqrt(msk + EPS) * kg_ref[...]
        kb_ref[rows, :] = kn.astype(BF16)
        zv = _dot(h, w_ref[:, q0 + 2 * qk_w:])
        for out_ref, val in ((k_ref, kn), (v_ref, zv)):
            if has_fill:
                out_ref[rows, :] = val
            else:
                heads = val.shape[1] // DV
                for hh in range(heads):
                    out_ref[pl.ds(r0 * heads + hh, sub, stride=heads), :] = val[:, hh * DV:(hh + 1) * DV]
        vt_ref[:, rows] = zv.T.astype(BF16)


def _inproj(x, sh, sc, ng, w_in, cw, qg, kg, bd, fills, tm):
    nb, t, d = x.shape
    d_a = cw.shape[1]
    qk_w = qg.shape[1]
    d_b = w_in.shape[1] - 3 * d_a - 2 * qk_w
    has_fill = fills is not None
    in_specs = [
        _rows_spec(tm, d), _mod_spec(sh, tm, d), _mod_spec(sc, tm, d),
        _const_spec(ng.shape), _const_spec(w_in.shape), _const_spec(cw.shape),
        _const_spec(qg.shape), _const_spec(kg.shape), _const_spec(bd.shape),
    ]
    args = [x, sh.mods, sc.mods, ng, w_in, cw, qg, kg, bd]
    scratch = []
    if has_fill:
        in_specs += [_rows_spec(tm, d_a), _rows_spec(tm, d_a)]
        args += list(fills)
        tail_spec = _rows_spec(tm, d_a)
        tail_shape = (nb, t, d_a)
    else:
        scratch = [pltpu.VMEM((8, d_a), F32)]
        tail_spec = pl.BlockSpec((None, 8, d_a), lambda b, i: (b, 0, 0))
        tail_shape = (nb, 8, d_a)
    if has_fill:
        kv_specs = [_rows_spec(tm, qk_w), _rows_spec(tm, d_b)]
        kv_shapes = [(nb, t, qk_w), (nb, t, d_b)]
    else:
        kv_specs = [_rows_spec(tm * (qk_w // DV), DV), _rows_spec(tm * (d_b // DV), DV)]
        kv_shapes = [(nb, t * (qk_w // DV), DV), (nb, t * (d_b // DV), DV)]
    out_specs = [_rows_spec(tm, d_a), _rows_spec(tm, qk_w), *kv_specs,
                 _rows_spec(tm, qk_w), pl.BlockSpec((None, d_b, tm), lambda b, i: (b, 0, i)), tail_spec]
    out_shape = [
        jax.ShapeDtypeStruct((nb, t, d_a), BF16), jax.ShapeDtypeStruct((nb, t, qk_w), BF16),
        jax.ShapeDtypeStruct(kv_shapes[0], F32), jax.ShapeDtypeStruct(kv_shapes[1], F32),
        jax.ShapeDtypeStruct((nb, t, qk_w), BF16), jax.ShapeDtypeStruct((nb, d_b, t), BF16),
        jax.ShapeDtypeStruct(tail_shape, F32),
    ]
    return pl.pallas_call(
        functools.partial(_inproj_kernel, tm=tm, n_sub=1 if has_fill else INPROJ_SUBTILES, d_a=d_a,
                          qk_w=qk_w, has_fill=has_fill, period=8),
        grid=(nb, t // tm),
        in_specs=in_specs, out_specs=out_specs, out_shape=out_shape,
        scratch_shapes=scratch,
        compiler_params=_cparams(2),
        name="inproj",
    )(*args)


def _diff_lambda(lam_ref, lam_init):
    lv = lam_ref[...]
    e1 = jnp.exp(jnp.sum(lv[0:1, :] * lv[1:2, :], axis=-1, keepdims=True))
    e2 = jnp.exp(jnp.sum(lv[2:3, :] * lv[3:4, :], axis=-1, keepdims=True))
    return e1 - e2 + lam_init


def _subln(o, g, lam_init):
    ms = jnp.mean(o * o, axis=-1, keepdims=True)
    return o * lax.rsqrt(ms + EPS) * g * (1.0 - lam_init)


def _softmax_step(s, v_dot, m_ref, l_ref, acc_ref, rows=slice(None)):
    m_prev = m_ref[rows, :]
    m_new = jnp.maximum(m_prev, jnp.max(s, axis=-1, keepdims=True))
    alpha = jnp.exp2(m_prev - m_new)
    p = jnp.exp2(s - m_new)
    l_ref[rows, :] = alpha * l_ref[rows, :] + jnp.sum(p, axis=-1, keepdims=True)
    acc_ref[rows, :] = alpha * acc_ref[rows, :] + v_dot(p.astype(BF16))
    m_ref[rows, :] = m_new


def _attn_kernel(q_ref, k_ref, vt_ref, lam_ref, sg_ref, o_ref, qs_ref, sa_ref, sb_ref, m_ref, acc_ref,
                 *, tq, hps, lam_init):
    i = pl.program_id(2)
    lane = lax.broadcasted_iota(jnp.int32, (tq, DV), 1)
    for hh in range(hps):
        q = q_ref[:, hh * DV:(hh + 1) * DV]
        zero = jnp.zeros_like(q)
        qs_ref[hh, 0:tq, :] = jnp.where(lane < DK, q, zero)
        qs_ref[hh, tq:2 * tq, :] = jnp.where(lane >= DK, q, zero)
    m_ref[...] = jnp.full_like(m_ref, -jnp.inf)
    acc_ref[...] = jnp.zeros_like(acc_ref)

    chains = [(hh, c0, hh * 2 * tq + c0) for hh in range(hps) for c0 in range(0, 2 * tq, ATTN_COLS)]
    ones = jnp.ones((acc_ref.shape[0] - DV, tq), BF16)

    def scores(j, s_ref):
        keys = pl.ds(pl.multiple_of(j * tq, tq), tq)
        for hh, c0, g0 in chains:
            k = k_ref[keys, hh * DV:(hh + 1) * DV]
            s_ref[:, g0:g0 + ATTN_COLS] = _dot_nt(k, qs_ref[hh, c0:c0 + ATTN_COLS, :])

    def step(j, s_ref, diagonal):
        keys = pl.ds(pl.multiple_of(j * tq, tq), tq)
        vts = [jnp.concatenate([vt_ref[hh * DV:(hh + 1) * DV, keys], ones], axis=0)
               for hh in range(hps)]
        for hh, c0, g0 in chains:
            cols = slice(g0, g0 + ATTN_COLS)
            st = s_ref[:, cols]
            if diagonal:
                key = lax.broadcasted_iota(jnp.int32, st.shape, 0)
                qrow = lax.broadcasted_iota(jnp.int32, st.shape, 1) + (c0 % tq)
                st = jnp.where(key <= qrow, st, -jnp.inf)
            m_prev = m_ref[:, cols]
            m_new = jnp.maximum(m_prev, jnp.max(st, axis=0, keepdims=True))
            alpha = jnp.exp2(m_prev - m_new)
            pt = jnp.exp2(st - m_new)
            acc_ref[:, cols] = alpha * acc_ref[:, cols] + _dot(vts[hh], pt.astype(BF16))
            m_ref[:, cols] = m_new

    scores(0, sa_ref)

    def pair(jj, carry):
        j0 = 2 * jj
        scores(j0 + 1, sb_ref)
        step(j0, sa_ref, False)
        scores(j0 + 2, sa_ref)
        step(j0 + 1, sb_ref, False)
        return carry

    lax.fori_loop(0, i // 2, pair, 0)

    @pl.when(i % 2 == 0)
    def _():
        step(i, sa_ref, True)

    @pl.when(i % 2 == 1)
    def _():
        scores(i, sb_ref)
        step(i - 1, sa_ref, False)
        step(i, sb_ref, True)

    lam = _diff_lambda(lam_ref, lam_init)
    for hh in range(hps):
        h0 = hh * 2 * tq
        nt = acc_ref[0:DV, h0:h0 + 2 * tq] / acc_ref[DV:DV + 1, h0:h0 + 2 * tq]
        ot = nt[:, 0:tq] - lam * nt[:, tq:2 * tq]
        ms = jnp.mean(ot * ot, axis=0, keepdims=True)
        ot = ot * lax.rsqrt(ms + EPS)
        o_ref[:, hh * DV:(hh + 1) * DV] = (ot.T * sg_ref[...] * (1.0 - lam_init)).astype(o_ref.dtype)


def _attn(q, kb, vt, lamv, sg, lam_init, tq, hps):
    b, t, w = q.shape
    hw = hps * DV
    return pl.pallas_call(
        functools.partial(_attn_kernel, tq=tq, hps=hps, lam_init=lam_init),
        grid=(b, w // hw, t // tq),
        in_specs=[
            pl.BlockSpec((None, tq, hw), lambda b_, h, i: (b_, i, h)),
            pl.BlockSpec((None, t, hw), lambda b_, h, i: (b_, 0, h)),
            pl.BlockSpec((None, hw, t), lambda b_, h, i: (b_, h, 0)),
            _const_spec(lamv.shape), _const_spec(sg.shape),
        ],
        out_specs=pl.BlockSpec((None, tq, hw), lambda b_, h, i: (b_, i, h)),
        out_shape=jax.ShapeDtypeStruct((b, t, w), BF16),
        scratch_shapes=[
            pltpu.VMEM((hps, 2 * tq, DV), BF16),
            pltpu.VMEM((tq, hps * 2 * tq), F32), pltpu.VMEM((tq, hps * 2 * tq), F32),
            pltpu.VMEM((1, hps * 2 * tq), F32),
            pltpu.VMEM((DV + BF16_ROWS, hps * 2 * tq), F32),
        ],
        compiler_params=_cparams(3),
        name="attn",
    )(q, kb, vt, lamv, sg)


def _sattn_kernel(pt_ref, q_ref, kn_ref, vn_ref, lam_ref, sg_ref, *rest, pp, ts, lam_init):
    k_refs = rest[0:pp]
    v_refs = rest[pp:2 * pp]
    o_ref, qs_ref, m_ref, l_ref, acc_ref = rest[2 * pp:]
    j = pl.program_id(1)
    rows = qs_ref.shape[0]
    hr = 2 * ts
    heads = rows // hr
    page = k_refs[0].shape[0] // heads

    @pl.when(j == 0)
    def _():
        q = q_ref[...]
        lane = lax.broadcasted_iota(jnp.int32, (ts, DV), 1)
        parts = []
        for h in range(heads):
            qh = q[:, h * DV:(h + 1) * DV]
            parts += [jnp.where(lane < DK, qh, 0.0), jnp.where(lane >= DK, qh, 0.0)]
        qs = jnp.concatenate(parts, axis=0).astype(BF16)
        qs_ref[...] = qs
        m_ref[...] = jnp.full_like(m_ref, -jnp.inf)
        l_ref[...] = jnp.zeros_like(l_ref)
        acc_ref[...] = jnp.zeros_like(acc_ref)
        pad = jnp.zeros((DV - ts, DV), F32)
        kns = [jnp.concatenate([kn_ref[:, h * DV:(h + 1) * DV], pad], axis=0).astype(BF16)
               for h in range(heads)]
        vns = [jnp.concatenate([vn_ref[:, h * DV:(h + 1) * DV], pad], axis=0).astype(BF16)
               for h in range(heads)]
        s = jnp.concatenate([_dot_nt(qs[h * hr:(h + 1) * hr, :], kns[h]) for h in range(heads)],
                            axis=0)
        rr = lax.broadcasted_iota(jnp.int32, s.shape, 0)
        cc = lax.broadcasted_iota(jnp.int32, s.shape, 1)
        s = jnp.where(cc <= jnp.bitwise_and(rr, ts - 1), s, -jnp.inf)

        def new_v_dot(pb):
            return jnp.concatenate([_dot(pb[h * hr:(h + 1) * hr, :], vns[h]) for h in range(heads)],
                                   axis=0)

        _softmax_step(s, new_v_dot, m_ref, l_ref, acc_ref)

    def head_rows(refs, h):
        return jnp.concatenate([r[pl.ds(h, page, stride=heads), :] for r in refs],
                               axis=0).astype(BF16)

    hrows = [slice(h * hr, (h + 1) * hr) for h in range(heads)]
    ss = [_dot_nt(qs_ref[hrows[h], :], head_rows(k_refs, h)) for h in range(heads)]
    for h in range(heads):
        _softmax_step(ss[h], lambda pb: _dot(pb, head_rows(v_refs, h)), m_ref, l_ref, acc_ref,
                      hrows[h])

    @pl.when(j == pl.num_programs(1) - 1)
    def _():
        lam = _diff_lambda(lam_ref, lam_init)
        n = acc_ref[...] / l_ref[...]
        outs = []
        for h in range(heads):
            o = n[h * hr:h * hr + ts, :] - lam * n[h * hr + ts:(h + 1) * hr, :]
            outs.append(_subln(o, sg_ref[...], lam_init))
        o_ref[...] = jnp.concatenate(outs, axis=1)


def _sattn(page_table, q, kn, vn, lamv, sg, cache_k, cache_v, layer, lam_init, pp):
    bs, ts, w = q.shape
    n_pages = page_table.shape[1]
    prow = cache_k.shape[2] * cache_k.shape[3]
    ck = cache_k.reshape(cache_k.shape[0], cache_k.shape[1], prow, DV)
    cv = cache_v.reshape(cache_v.shape[0], cache_v.shape[1], prow, DV)
    rows = (w // DK) * ts

    def page_spec(p):
        return pl.BlockSpec((None, None, prow, DV),
                            lambda b, j, pt: (layer, pt[b, j * pp + p], 0, 0))

    new_spec = pl.BlockSpec((None, ts, w), lambda b, j, pt: (b, 0, 0))
    grid_spec = pltpu.PrefetchScalarGridSpec(
        num_scalar_prefetch=1,
        grid=(bs, n_pages // pp),
        in_specs=[new_spec, new_spec, new_spec, _const_spec(lamv.shape), _const_spec(sg.shape)]
        + [page_spec(p) for p in range(pp)] + [page_spec(p) for p in range(pp)],
        out_specs=new_spec,
        scratch_shapes=[
            pltpu.VMEM((rows, DV), BF16),
            pltpu.VMEM((rows, 1), F32), pltpu.VMEM((rows, 1), F32),
            pltpu.VMEM((rows, DV), F32),
        ],
    )
    return pl.pallas_call(
        functools.partial(_sattn_kernel, pp=pp, ts=ts, lam_init=lam_init),
        grid_spec=grid_spec,
        out_shape=jax.ShapeDtypeStruct((bs, ts, w), F32),
        compiler_params=_cparams(2),
        name="sattn",
    )(page_table, q, kn, vn, lamv, sg, *([ck] * pp), *([cv] * pp))


def _post_even_kernel(x_ref, a_ref, o_ref, g_ref, wo_ref, ng_ref, shf_ref, scf_ref, gf_ref,
                      wgu_ref, wd_ref, y_ref):
    d_a = a_ref.shape[1]
    m = _dot(a_ref[...], wo_ref[0:d_a, :]) + _dot(o_ref[...].astype(BF16), wo_ref[d_a:, :])
    x1 = x_ref[...] + g_ref[...] * m
    y_ref[...] = _ffn(x1, ng_ref[...], shf_ref[...], scf_ref[...], gf_ref[...], wgu_ref, wd_ref)


def _post_even(x, a, o, g, w_out, ng, shf, scf, gf, wgu, wd, tm):
    nb, t, d = x.shape
    return pl.pallas_call(
        _post_even_kernel,
        grid=(nb, t // tm),
        in_specs=[
            _rows_spec(tm, d), _rows_spec(tm, a.shape[2]), _rows_spec(tm, o.shape[2]),
            _mod_spec(g, tm, d), _const_spec(w_out.shape), _const_spec(ng.shape),
            _mod_spec(shf, tm, d), _mod_spec(scf, tm, d), _mod_spec(gf, tm, d),
            _layer_spec(wgu), _layer_spec(wd),
        ],
        out_specs=_rows_spec(tm, d),
        out_shape=jax.ShapeDtypeStruct((nb, t, d), F32),
        compiler_params=_cparams(2),
        name="post_even",
    )(x, a, o, g.mods, w_out, ng, shf.mods, scf.mods, gf.mods, wgu.stack, wd.stack)


def _odd_kernel(*refs, tm, short_rows, emit_v):
    (x_ref, sh_ref, sc_ref, ng0_ref, wi_ref, bi_ref, lg_ref, lb_ref, ws_ref, bimg_ref, wo_ref,
     g_ref, ng1_ref, shf_ref, scf_ref, gf_ref, wgu_ref, wd_ref) = refs[:18]
    if emit_v:
        y_ref, v_ref, mix_ref = refs[18:]
    else:
        y_ref, mix_ref = refs[18:]
    d_c = wo_ref.shape[0]
    gw = d_c // G_C

    x = x_ref[...]
    h = _rms_mod(x, ng0_ref[...], sh_ref[...], sc_ref[...]).astype(BF16)
    u = _gelu_tanh(_dot(h, wi_ref[:, 0:d_c]) + bi_ref[:, 0:d_c])
    v = _gelu_tanh(_dot(h, wi_ref[:, d_c:]) + bi_ref[:, d_c:])
    mu = jnp.mean(v, axis=-1, keepdims=True)
    vc = v - mu
    var = jnp.mean(vc * vc, axis=-1, keepdims=True)
    vn = vc * lax.rsqrt(var + EPS) * lg_ref[...] + lb_ref[...]
    if emit_v:
        v_ref[...] = vn
    vb = vn.astype(BF16)

    r = lax.broadcasted_iota(jnp.int32, (CHUNK, CHUNK), 0)
    c = lax.broadcasted_iota(jnp.int32, (CHUNK, CHUNK), 1)
    if short_rows is None:
        for gi in range(G_C):
            wsg = jnp.where(c <= r, ws_ref[gi], 0.0).astype(BF16)
            for n in range(tm // CHUNK):
                blk = _dot(wsg, vb[n * CHUNK:(n + 1) * CHUNK, gi * gw:(gi + 1) * gw])
                mix_ref[n * CHUNK:(n + 1) * CHUNK, gi * gw:(gi + 1) * gw] = (
                    blk + bimg_ref[:, gi * gw:(gi + 1) * gw])
    else:
        rr = lax.broadcasted_iota(jnp.int32, (tm, CHUNK), 0)
        cc = lax.broadcasted_iota(jnp.int32, (tm, CHUNK), 1)
        sel = jnp.where(jnp.bitwise_and(rr, short_rows - 1) == cc, 1.0, 0.0).astype(BF16)
        r2 = lax.broadcasted_iota(jnp.int32, (tm, tm), 0)
        c2 = lax.broadcasted_iota(jnp.int32, (tm, tm), 1)
        same_seq = _shr(r2, short_rows) == _shr(c2, short_rows)
        causal = jnp.bitwise_and(c2, short_rows - 1) <= jnp.bitwise_and(r2, short_rows - 1)
        keep = jnp.where(same_seq, jnp.where(causal, 1.0, 0.0), 0.0)
        for gi in range(G_C):
            spread = _dot_nt(_dot(sel, ws_ref[gi].astype(BF16)).astype(BF16), sel)
            mg = (spread * keep).astype(BF16)
            blk = _dot(mg, vb[:, gi * gw:(gi + 1) * gw])
            mix_ref[:, gi * gw:(gi + 1) * gw] = blk + bimg_ref[:, gi * gw:(gi + 1) * gw]

    y = _dot((u * mix_ref[...]).astype(BF16), wo_ref[...])
    x1 = x + g_ref[...] * y
    y_ref[...] = _ffn(x1, ng1_ref[...], shf_ref[...], scf_ref[...], gf_ref[...], wgu_ref, wd_ref)


def _odd(x, sh, sc, ng0, w_in, b_in, ln_g, ln_b, w_s, bimg, w_out, g, ng1, shf, scf, gf, wgu, wd,
         tm, short_rows, emit_v):
    nb, t, d = x.shape
    d_c = w_out.shape[0]
    out_specs = [_rows_spec(tm, d)]
    out_shape = [jax.ShapeDtypeStruct((nb, t, d), F32)]
    if emit_v:
        out_specs.append(_rows_spec(tm, d_c))
        out_shape.append(jax.ShapeDtypeStruct((nb, t, d_c), F32))
    return pl.pallas_call(
        functools.partial(_odd_kernel, tm=tm, short_rows=short_rows, emit_v=emit_v),
        grid=(nb, t // tm),
        in_specs=[
            _rows_spec(tm, d), _mod_spec(sh, tm, d), _mod_spec(sc, tm, d), _const_spec(ng0.shape),
            _const_spec(w_in.shape), _const_spec(b_in.shape), _const_spec(ln_g.shape),
            _const_spec(ln_b.shape), _const_spec(w_s.shape), _const_spec(bimg.shape),
            _const_spec(w_out.shape), _mod_spec(g, tm, d), _const_spec(ng1.shape),
            _mod_spec(shf, tm, d), _mod_spec(scf, tm, d), _mod_spec(gf, tm, d),
            _layer_spec(wgu), _layer_spec(wd),
        ],
        out_specs=out_specs, out_shape=out_shape,
        scratch_shapes=[pltpu.VMEM((tm, d_c), F32)],
        compiler_params=_cparams(2),
        name="odd",
    )(x, sh.mods, sc.mods, ng0, w_in, b_in, ln_g, ln_b, w_s, bimg, w_out, g.mods, ng1, shf.mods,
      scf.mods, gf.mods, wgu.stack, wd.stack)


def kernel(x_prompt, x_sample, c_prompt, c_sample, cache_k, cache_v, state_conv, page_table, w_ada, b_ada, norm_g, w_in_even, conv_w, q_norm_g, k_norm_g, lam_q1, lam_k1, lam_q2, lam_k2, subln_g, w_out_even, w_in_odd, b_in_odd, ln_g, ln_b, w_s, b_s, w_out_odd, w_gate_up, w_down):
    bp, tp, d = x_prompt.shape
    bs, ts, _ = x_sample.shape
    depth = w_ada.shape[0]
    d_a = conv_w.shape[2]
    qk_w = H_B * 2 * DK
    rows_s = bs * ts

    mods = _adaln(jnp.concatenate([c_prompt, c_sample], axis=0), w_ada, b_ada)

    grp = jnp.arange(MXU_TILE) // DK
    bd = jnp.where(grp[:, None] == grp[None, :], 1.0 / DK, 0.0).astype(BF16)

    wgu_all = w_gate_up.astype(BF16)
    wd_all = w_down.astype(BF16)
    xp = x_prompt
    xs = x_sample.reshape(1, rows_s, d)
    k_p, v_p, conv_p, k_s, v_s, conv_s, chunk_v = [], [], [], [], [], [], []

    for i in range(depth):
        mods_p = mods[i, :bp].reshape(bp, 1, -1)
        mods_s = jnp.repeat(mods[i, bp:], ts, axis=0).reshape(1, rows_s, -1)
        mp = [_ModChunk(mods_p, c) for c in range(6)]
        ms = [_ModChunk(mods_s, c) for c in range(6)]
        ng0 = norm_g[i, 0].reshape(1, d)
        ng1 = norm_g[i, 1].reshape(1, d)
        wgu = _LayerOf(wgu_all, i)
        wd = _LayerOf(wd_all, i)
        if i % 2 == 0:
            e = i // 2
            lam_init = 0.8 - 0.6 * math.exp(-0.3 * i)
            w_in = w_in_even[e].astype(BF16)
            w_out = w_out_even[e].astype(BF16)
            qg = jnp.tile(q_norm_g[e], qk_w // DK).reshape(1, qk_w)
            kg = jnp.tile(k_norm_g[e], qk_w // DK).reshape(1, qk_w)
            lamv = jnp.stack([lam_q1[e], lam_k1[e], lam_q2[e], lam_k2[e]])
            sg = subln_g[e].reshape(1, DV)

            a, q, k, v, kb, vt, tail = _inproj(xp, mp[0], mp[1], ng0, w_in, conv_w[e], qg, kg, bd,
                                               None, INPROJ_TILE)
            o = _attn(q, kb, vt, lamv, sg, lam_init, ATTN_TILE, ATTN_HEADS)
            xp = _post_even(xp, a, o, mp[2], w_out, ng1, mp[3], mp[4], mp[5], wgu, wd, ROW_TILE)
            k_p.append(k.reshape(bp, tp, H_B, 2 * DK))
            v_p.append(v.reshape(bp, tp, H_B, DV))
            conv_p.append(tail[:, 8 - (CONV_W - 1):, :])

            st = state_conv[e]
            zeros = jnp.zeros((bs, ts - 1, d_a), F32)
            fill1 = jnp.concatenate([st[:, 1:2], zeros], axis=1).reshape(1, rows_s, d_a)
            fill2 = jnp.concatenate([st, zeros[:, 1:]], axis=1).reshape(1, rows_s, d_a)
            a, q, k, v, _, _, xc = _inproj(xs, ms[0], ms[1], ng0, w_in, conv_w[e], qg, kg, bd,
                                           (fill1, fill2), rows_s)
            o = _sattn(page_table, q.astype(F32).reshape(bs, ts, qk_w), k.reshape(bs, ts, qk_w),
                       v.reshape(bs, ts, -1), lamv, sg, cache_k, cache_v, e, lam_init,
                       PAGES_PER_STEP)
            xs = _post_even(xs, a, o.reshape(1, rows_s, -1), ms[2], w_out, ng1, ms[3], ms[4], ms[5],
                            wgu, wd, rows_s)
            k_s.append(k.reshape(bs, ts, H_B, 2 * DK))
            v_s.append(v.reshape(bs, ts, H_B, DV))
            conv_s.append(xc.reshape(bs, ts, d_a)[:, ts - (CONV_W - 1):, :])
        else:
            oi = i // 2
            d_c = w_out_odd.shape[1]
            w_in = w_in_odd[oi].astype(BF16)
            w_out = w_out_odd[oi].astype(BF16)
            b_in = b_in_odd[oi].reshape(1, -1)
            lg = ln_g[oi].reshape(1, d_c)
            lb = ln_b[oi].reshape(1, d_c)
            bimg = jnp.repeat(b_s[oi].T, d_c // G_C, axis=1)
            xp = _odd(xp, mp[0], mp[1], ng0, w_in, b_in, lg, lb, w_s[oi], bimg, w_out, mp[2], ng1,
                      mp[3], mp[4], mp[5], wgu, wd, ROW_TILE, None, False)[0]
            bimg_s = jnp.tile(bimg[:ts], (bs, 1))
            xs, vrows = _odd(xs, ms[0], ms[1], ng0, w_in, b_in, lg, lb, w_s[oi], bimg_s, w_out, ms[2],
                             ng1, ms[3], ms[4], ms[5], wgu, wd, rows_s, ts, True)
            chunk_v.append(vrows.reshape(bs, ts, d_c))

    return (xp, xs.reshape(bs, ts, d), jnp.stack(k_p), jnp.stack(v_p), jnp.stack(conv_p),
            jnp.stack(k_s), jnp.stack(v_s), jnp.stack(conv_s), jnp.stack(chunk_v))
```

```python
import functools
import math
from typing import NamedTuple

import jax
import jax.numpy as jnp
from jax import lax
from jax.experimental import pallas as pl
from jax.experimental.pallas import tpu as pltpu

F32 = jnp.float32
BF16 = jnp.bfloat16
EPS = 1e-6

A_GROUPS = 8
CONV_W = 3
H_B = 4
DK = 64
DV = 2 * DK
CHUNK = 128
G_C = 8
Q_SCALE = DK ** -0.5 * math.log2(math.e)

VMEM_LIMIT_BYTES = 56 * 1024 * 1024
MXU_TILE = 256
BF16_ROWS = 16

ROW_TILE = 512
INPROJ_TILE = 1024
INPROJ_SUBTILES = 4
ATTN_TILE = 512
ATTN_COLS = 512
ATTN_HEADS = 4
PAGES_PER_STEP = 32


def _cparams(n_axes):
    return pltpu.CompilerParams(
        dimension_semantics=("arbitrary",) * n_axes,
        vmem_limit_bytes=VMEM_LIMIT_BYTES)


def _const_spec(shape):
    zeros = (0,) * len(shape)
    return pl.BlockSpec(shape, lambda *_: zeros, pipeline_mode=pl.Buffered(1))


class _LayerOf(NamedTuple):
    stack: jax.Array
    layer: int


def _layer_spec(w):
    shape = w.stack.shape
    index = (w.layer,) + (0,) * (len(shape) - 1)
    return pl.BlockSpec((None,) + shape[1:], lambda *_: index, pipeline_mode=pl.Buffered(1))


class _ModChunk(NamedTuple):
    mods: jax.Array
    chunk: int


def _mod_spec(m, tm, d):
    chunk = m.chunk
    if m.mods.shape[1] == 1:
        return pl.BlockSpec((None, 1, d), lambda b, i: (b, 0, chunk))
    return pl.BlockSpec((None, tm, d), lambda b, i: (b, i, chunk))


def _rows_spec(tm, width):
    return pl.BlockSpec((None, tm, width), lambda b, i: (b, i, 0))


def _dot(a, b):
    return jnp.dot(a, b, preferred_element_type=F32)


def _dot_nt(a, b):
    return lax.dot_general(a, b, (((1,), (1,)), ((), ())), preferred_element_type=F32)


def _mod_rows(ref, rows):
    return ref[...] if ref.shape[0] == 1 else ref[rows, :]


def _rms_mod(x, g, shift, scale):
    ms = jnp.mean(x * x, axis=-1, keepdims=True)
    y = x * lax.rsqrt(ms + EPS) * g
    return y * (1.0 + scale) + shift


def _shr(x, pow2):
    shift = pow2.bit_length() - 1
    assert 1 << shift == pow2
    return jnp.right_shift(x, shift)


def _silu(x):
    return x / (1.0 + jnp.exp(-x))


def _gelu_tanh(x):
    c = math.sqrt(2.0 / math.pi)
    half = 0.5 * x
    return half + half * jnp.tanh(x * (c + (c * 0.044715) * (x * x)))


def _ffn(x1, ng, shf, scf, gf, wgu_ref, wd_ref):
    d_ff = wd_ref.shape[0]
    n_tiles, rem = divmod(d_ff, MXU_TILE)
    assert rem == 0
    split = (n_tiles + 1) // 2 * MXU_TILE
    h = _rms_mod(x1, ng, shf, scf).astype(BF16)
    acc = None
    for lo, hi in ((0, split), (split, d_ff)):
        g = _dot(h, wgu_ref[:, lo:hi])
        u = _dot(h, wgu_ref[:, d_ff + lo:d_ff + hi])
        a = (_silu(g) * u).astype(BF16)
        d = _dot(a, wd_ref[lo:hi, :])
        acc = d if acc is None else acc + d
    return x1 + gf * acc


def _adaln_kernel(c_ref, w_ref, b_ref, o_ref):
    s = _silu(c_ref[...])
    o_ref[...] = jnp.dot(s, w_ref[...], precision=lax.Precision.HIGHEST,
                         preferred_element_type=F32) + b_ref[...]


def _adaln(c_all, w_ada, b_ada):
    depth, d, d6 = w_ada.shape
    r = c_all.shape[0]
    tn = d6 // 4
    return pl.pallas_call(
        _adaln_kernel,
        grid=(depth, d6 // tn),
        in_specs=[
            pl.BlockSpec((r, d), lambda l, j: (0, 0)),
            pl.BlockSpec((None, d, tn), lambda l, j: (l, 0, j)),
            pl.BlockSpec((None, 1, tn), lambda l, j: (l, 0, j)),
        ],
        out_specs=pl.BlockSpec((None, r, tn), lambda l, j: (l, 0, j)),
        out_shape=jax.ShapeDtypeStruct((depth, r, d6), F32),
        compiler_params=_cparams(2),
        name="adaln",
    )(c_all, w_ada, b_ada.reshape(depth, 1, d6))


def _group_mean_sq(z, bd_ref):
    sq = (z * z).astype(BF16)
    w = bd_ref.shape[0]
    return jnp.concatenate([_dot(sq[:, c:c + w], bd_ref[...]) for c in range(0, z.shape[1], w)],
                           axis=1)


def _inproj_kernel(*refs, tm, n_sub, d_a, qk_w, has_fill, period):
    if has_fill:
        (x_ref, sh_ref, sc_ref, ng_ref, w_ref, cw_ref, qg_ref, kg_ref, bd_ref, f1_ref, f2_ref,
         a_ref, q_ref, k_ref, v_ref, kb_ref, vt_ref, tail_ref) = refs
    else:
        (x_ref, sh_ref, sc_ref, ng_ref, w_ref, cw_ref, qg_ref, kg_ref, bd_ref,
         a_ref, q_ref, k_ref, v_ref, kb_ref, vt_ref, tail_ref, carry_ref) = refs

    if not has_fill:
        @pl.when(pl.program_id(1) == 0)
        def _():
            carry_ref[...] = jnp.zeros_like(carry_ref)

    sub = tm // n_sub
    for r0 in range(0, tm, sub):
        rows = slice(r0, r0 + sub)
        h = _rms_mod(x_ref[rows, :], ng_ref[...], _mod_rows(sh_ref, rows),
                     _mod_rows(sc_ref, rows)).astype(BF16)

        a_b = _dot(h, w_ref[:, 0:d_a])
        xc = _dot(h, w_ref[:, d_a:2 * d_a]) * _dot(h, w_ref[:, 2 * d_a:3 * d_a])
        row = lax.broadcasted_iota(jnp.int32, (sub, d_a), 0)
        if has_fill:
            rp = jnp.bitwise_and(row, period - 1)
            fill1 = f1_ref[rows, :]
            fill2 = f2_ref[rows, :]
        else:
            rp = row
            last1 = carry_ref[pl.ds(7, 1), :]
            last2 = carry_ref[pl.ds(6, 1), :]
            fill1 = jnp.broadcast_to(last1, (sub, d_a))
            fill2 = jnp.where(row == 0, last2, last1)
        prev1 = jnp.where(rp < 1, fill1, pltpu.roll(xc, 1, 0))
        prev2 = jnp.where(rp < 2, fill2, pltpu.roll(xc, 2, 0))
        y = cw_ref[0:1, :] * prev2 + cw_ref[1:2, :] * prev1 + cw_ref[2:3, :] * xc
        a_ref[rows, :] = (a_b * y).astype(BF16)
        if has_fill:
            tail_ref[rows, :] = xc
        else:
            carry_ref[...] = xc[sub - 8:sub, :]
            if r0 + sub == tm:
                tail_ref[...] = xc[sub - 8:sub, :]

        q0 = 3 * d_a
        zq = _dot(h, w_ref[:, q0:q0 + qk_w])
        msq = _group_mean_sq(zq, bd_ref)
        qn = zq * lax.rsqrt(msq + EPS) * qg_ref[...]
        q_ref[rows, :] = (qn * Q_SCALE).astype(BF16)
        zk = _dot(h, w_ref[:, q0 + qk_w:q0 + 2 * qk_w])
        msk = _group_mean_sq(zk, bd_ref)
        kn = zk * lax.rsqrt(msk + EPS) * kg_ref[...]
        kb_ref[rows, :] = kn.astype(BF16)
        zv = _dot(h, w_ref[:, q0 + 2 * qk_w:])
        for out_ref, val in ((k_ref, kn), (v_ref, zv)):
            if has_fill:
                out_ref[rows, :] = val
            else:
                heads = val.shape[1] // DV
                for hh in range(heads):
                    out_ref[pl.ds(r0 * heads + hh, sub, stride=heads), :] = val[:, hh * DV:(hh + 1) * DV]
        vt_ref[:, rows] = zv.T.astype(BF16)


def _inproj(x, sh, sc, ng, w_in, cw, qg, kg, bd, fills, tm):
    nb, t, d = x.shape
    d_a = cw.shape[1]
    qk_w = qg.shape[1]
    d_b = w_in.shape[1] - 3 * d_a - 2 * qk_w
    has_fill = fills is not None
    in_specs = [
        _rows_spec(tm, d), _mod_spec(sh, tm, d), _mod_spec(sc, tm, d),
        _const_spec(ng.shape), _const_spec(w_in.shape), _const_spec(cw.shape),
        _const_spec(qg.shape), _const_spec(kg.shape), _const_spec(bd.shape),
    ]
    args = [x, sh.mods, sc.mods, ng, w_in, cw, qg, kg, bd]
    scratch = []
    if has_fill:
        in_specs += [_rows_spec(tm, d_a), _rows_spec(tm, d_a)]
        args += list(fills)
        tail_spec = _rows_spec(tm, d_a)
        tail_shape = (nb, t, d_a)
    else:
        scratch = [pltpu.VMEM((8, d_a), F32)]
        tail_spec = pl.BlockSpec((None, 8, d_a), lambda b, i: (b, 0, 0))
        tail_shape = (nb, 8, d_a)
    if has_fill:
        kv_specs = [_rows_spec(tm, qk_w), _rows_spec(tm, d_b)]
        kv_shapes = [(nb, t, qk_w), (nb, t, d_b)]
    else:
        kv_specs = [_rows_spec(tm * (qk_w // DV), DV), _rows_spec(tm * (d_b // DV), DV)]
        kv_shapes = [(nb, t * (qk_w // DV), DV), (nb, t * (d_b // DV), DV)]
    out_specs = [_rows_spec(tm, d_a), _rows_spec(tm, qk_w), *kv_specs,
                 _rows_spec(tm, qk_w), pl.BlockSpec((None, d_b, tm), lambda b, i: (b, 0, i)), tail_spec]
    out_shape = [
        jax.ShapeDtypeStruct((nb, t, d_a), BF16), jax.ShapeDtypeStruct((nb, t, qk_w), BF16),
        jax.ShapeDtypeStruct(kv_shapes[0], F32), jax.ShapeDtypeStruct(kv_shapes[1], F32),
        jax.ShapeDtypeStruct((nb, t, qk_w), BF16), jax.ShapeDtypeStruct((nb, d_b, t), BF16),
        jax.ShapeDtypeStruct(tail_shape, F32),
    ]
    return pl.pallas_call(
        functools.partial(_inproj_kernel, tm=tm, n_sub=1 if has_fill else INPROJ_SUBTILES, d_a=d_a,
                          qk_w=qk_w, has_fill=has_fill, period=8),
        grid=(nb, t // tm),
        in_specs=in_specs, out_specs=out_specs, out_shape=out_shape,
        scratch_shapes=scratch,
        compiler_params=_cparams(2),
        name="inproj",
    )(*args)


def _diff_lambda(lam_ref, lam_init):
    lv = lam_ref[...]
    e1 = jnp.exp(jnp.sum(lv[0:1, :] * lv[1:2, :], axis=-1, keepdims=True))
    e2 = jnp.exp(jnp.sum(lv[2:3, :] * lv[3:4, :], axis=-1, keepdims=True))
    return e1 - e2 + lam_init


def _subln(o, g, lam_init):
    ms = jnp.mean(o * o, axis=-1, keepdims=True)
    return o * lax.rsqrt(ms + EPS) * g * (1.0 - lam_init)


def _softmax_step(s, v_dot, m_ref, l_ref, acc_ref, rows=slice(None)):
    m_prev = m_ref[rows, :]
    m_new = jnp.maximum(m_prev, jnp.max(s, axis=-1, keepdims=True))
    alpha = jnp.exp2(m_prev - m_new)
    p = jnp.exp2(s - m_new)
    l_ref[rows, :] = alpha * l_ref[rows, :] + jnp.sum(p, axis=-1, keepdims=True)
    acc_ref[rows, :] = alpha * acc_ref[rows, :] + v_dot(p.astype(BF16))
    m_ref[rows, :] = m_new


def _attn_kernel(q_ref, k_ref, vt_ref, lam_ref, sg_ref, o_ref, qs_ref, sa_ref, sb_ref, m_ref, acc_ref,
                 *, tq, hps, lam_init):
    i = pl.program_id(2)
    lane = lax.broadcasted_iota(jnp.int32, (tq, DV), 1)
    for hh in range(hps):
        q = q_ref[:, hh * DV:(hh + 1) * DV]
        zero = jnp.zeros_like(q)
        qs_ref[hh, 0:tq, :] = jnp.where(lane < DK, q, zero)
        qs_ref[hh, tq:2 * tq, :] = jnp.where(lane >= DK, q, zero)
    m_ref[...] = jnp.full_like(m_ref, -jnp.inf)
    acc_ref[...] = jnp.zeros_like(acc_ref)

    chains = [(hh, c0, hh * 2 * tq + c0) for hh in range(hps) for c0 in range(0, 2 * tq, ATTN_COLS)]
    ones = jnp.ones((acc_ref.shape[0] - DV, tq), BF16)

    def scores(j, s_ref):
        keys = pl.ds(pl.multiple_of(j * tq, tq), tq)
        for hh, c0, g0 in chains:
            k = k_ref[keys, hh * DV:(hh + 1) * DV]
            s_ref[:, g0:g0 + ATTN_COLS] = _dot_nt(k, qs_ref[hh, c0:c0 + ATTN_COLS, :])

    def step(j, s_ref, diagonal):
        keys = pl.ds(pl.multiple_of(j * tq, tq), tq)
        vts = [jnp.concatenate([vt_ref[hh * DV:(hh + 1) * DV, keys], ones], axis=0)
               for hh in range(hps)]
        for hh, c0, g0 in chains:
            cols = slice(g0, g0 + ATTN_COLS)
            st = s_ref[:, cols]
            if diagonal:
                key = lax.broadcasted_iota(jnp.int32, st.shape, 0)
                qrow = lax.broadcasted_iota(jnp.int32, st.shape, 1) + (c0 % tq)
                st = jnp.where(key <= qrow, st, -jnp.inf)
            m_prev = m_ref[:, cols]
            m_new = jnp.maximum(m_prev, jnp.max(st, axis=0, keepdims=True))
            alpha = jnp.exp2(m_prev - m_new)
            pt = jnp.exp2(st - m_new)
            acc_ref[:, cols] = alpha * acc_ref[:, cols] + _dot(vts[hh], pt.astype(BF16))
            m_ref[:, cols] = m_new

    scores(0, sa_ref)

    def pair(jj, carry):
        j0 = 2 * jj
        scores(j0 + 1, sb_ref)
        step(j0, sa_ref, False)
        scores(j0 + 2, sa_ref)
        step(j0 + 1, sb_ref, False)
        return carry

    lax.fori_loop(0, i // 2, pair, 0)

    @pl.when(i % 2 == 0)
    def _():
        step(i, sa_ref, True)

    @pl.when(i % 2 == 1)
    def _():
        scores(i, sb_ref)
        step(i - 1, sa_ref, False)
        step(i, sb_ref, True)

    lam = _diff_lambda(lam_ref, lam_init)
    for hh in range(hps):
        h0 = hh * 2 * tq
        nt = acc_ref[0:DV, h0:h0 + 2 * tq] / acc_ref[DV:DV + 1, h0:h0 + 2 * tq]
        ot = nt[:, 0:tq] - lam * nt[:, tq:2 * tq]
        ms = jnp.mean(ot * ot, axis=0, keepdims=True)
        ot = ot * lax.rsqrt(ms + EPS)
        o_ref[:, hh * DV:(hh + 1) * DV] = (ot.T * sg_ref[...] * (1.0 - lam_init)).astype(o_ref.dtype)


def _attn(q, kb, vt, lamv, sg, lam_init, tq, hps):
    b, t, w = q.shape
    hw = hps * DV
    return pl.pallas_call(
        functools.partial(_attn_kernel, tq=tq, hps=hps, lam_init=lam_init),
        grid=(b, w // hw, t // tq),
        in_specs=[
            pl.BlockSpec((None, tq, hw), lambda b_, h, i: (b_, i, h)),
            pl.BlockSpec((None, t, hw), lambda b_, h, i: (b_, 0, h)),
            pl.BlockSpec((None, hw, t), lambda b_, h, i: (b_, h, 0)),
            _const_spec(lamv.shape), _const_spec(sg.shape),
        ],
        out_specs=pl.BlockSpec((None, tq, hw), lambda b_, h, i: (b_, i, h)),
        out_shape=jax.ShapeDtypeStruct((b, t, w), BF16),
        scratch_shapes=[
            pltpu.VMEM((hps, 2 * tq, DV), BF16),
            pltpu.VMEM((tq, hps * 2 * tq), F32), pltpu.VMEM((tq, hps * 2 * tq), F32),
            pltpu.VMEM((1, hps * 2 * tq), F32),
            pltpu.VMEM((DV + BF16_ROWS, hps * 2 * tq), F32),
        ],
        compiler_params=_cparams(3),
        name="attn",
    )(q, kb, vt, lamv, sg)


def _sattn_kernel(pt_ref, q_ref, kn_ref, vn_ref, lam_ref, sg_ref, *rest, pp, ts, lam_init):
    k_refs = rest[0:pp]
    v_refs = rest[pp:2 * pp]
    o_ref, qs_ref, m_ref, l_ref, acc_ref = rest[2 * pp:]
    j = pl.program_id(1)
    rows = qs_ref.shape[0]
    hr = 2 * ts
    heads = rows // hr
    page = k_refs[0].shape[0] // heads

    @pl.when(j == 0)
    def _():
        q = q_ref[...]
        lane = lax.broadcasted_iota(jnp.int32, (ts, DV), 1)
        parts = []
        for h in range(heads):
            qh = q[:, h * DV:(h + 1) * DV]
            parts += [jnp.where(lane < DK, qh, 0.0), jnp.where(lane >= DK, qh, 0.0)]
        qs = jnp.concatenate(parts, axis=0).astype(BF16)
        qs_ref[...] = qs
        m_ref[...] = jnp.full_like(m_ref, -jnp.inf)
        l_ref[...] = jnp.zeros_like(l_ref)
        acc_ref[...] = jnp.zeros_like(acc_ref)
        pad = jnp.zeros((DV - ts, DV), F32)
        kns = [jnp.concatenate([kn_ref[:, h * DV:(h + 1) * DV], pad], axis=0).astype(BF16)
               for h in range(heads)]
        vns = [jnp.concatenate([vn_ref[:, h * DV:(h + 1) * DV], pad], axis=0).astype(BF16)
               for h in range(heads)]
        s = jnp.concatenate([_dot_nt(qs[h * hr:(h + 1) * hr, :], kns[h]) for h in range(heads)],
                            axis=0)
        rr = lax.broadcasted_iota(jnp.int32, s.shape, 0)
        cc = lax.broadcasted_iota(jnp.int32, s.shape, 1)
        s = jnp.where(cc <= jnp.bitwise_and(rr, ts - 1), s, -jnp.inf)

        def new_v_dot(pb):
            return jnp.concatenate([_dot(pb[h * hr:(h + 1) * hr, :], vns[h]) for h in range(heads)],
                                   axis=0)

        _softmax_step(s, new_v_dot, m_ref, l_ref, acc_ref)

    def head_rows(refs, h):
        return jnp.concatenate([r[pl.ds(h, page, stride=heads), :] for r in refs],
                               axis=0).astype(BF16)

    hrows = [slice(h * hr, (h + 1) * hr) for h in range(heads)]
    ss = [_dot_nt(qs_ref[hrows[h], :], head_rows(k_refs, h)) for h in range(heads)]
    for h in range(heads):
        _softmax_step(ss[h], lambda pb: _dot(pb, head_rows(v_refs, h)), m_ref, l_ref, acc_ref,
                      hrows[h])

    @pl.when(j == pl.num_programs(1) - 1)
    def _():
        lam = _diff_lambda(lam_ref, lam_init)
        n = acc_ref[...] / l_ref[...]
        outs = []
        for h in range(heads):
            o = n[h * hr:h * hr + ts, :] - lam * n[h * hr + ts:(h + 1) * hr, :]
            outs.append(_subln(o, sg_ref[...], lam_init))
        o_ref[...] = jnp.concatenate(outs, axis=1)


def _sattn(page_table, q, kn, vn, lamv, sg, cache_k, cache_v, layer, lam_init, pp):
    bs, ts, w = q.shape
    n_pages = page_table.shape[1]
    prow = cache_k.shape[2] * cache_k.shape[3]
    ck = cache_k.reshape(cache_k.shape[0], cache_k.shape[1], prow, DV)
    cv = cache_v.reshape(cache_v.shape[0], cache_v.shape[1], prow, DV)
    rows = (w // DK) * ts

    def page_spec(p):
        return pl.BlockSpec((None, None, prow, DV),
                            lambda b, j, pt: (layer, pt[b, j * pp + p], 0, 0))

    new_spec = pl.BlockSpec((None, ts, w), lambda b, j, pt: (b, 0, 0))
    grid_spec = pltpu.PrefetchScalarGridSpec(
        num_scalar_prefetch=1,
        grid=(bs, n_pages // pp),
        in_specs=[new_spec, new_spec, new_spec, _const_spec(lamv.shape), _const_spec(sg.shape)]
        + [page_spec(p) for p in range(pp)] + [page_spec(p) for p in range(pp)],
        out_specs=new_spec,
        scratch_shapes=[
            pltpu.VMEM((rows, DV), BF16),
            pltpu.VMEM((rows, 1), F32), pltpu.VMEM((rows, 1), F32),
            pltpu.VMEM((rows, DV), F32),
        ],
    )
    return pl.pallas_call(
        functools.partial(_sattn_kernel, pp=pp, ts=ts, lam_init=lam_init),
        grid_spec=grid_spec,
        out_shape=jax.ShapeDtypeStruct((bs, ts, w), F32),
        compiler_params=_cparams(2),
        name="sattn",
    )(page_table, q, kn, vn, lamv, sg, *([ck] * pp), *([cv] * pp))


def _post_even_kernel(x_ref, a_ref, o_ref, g_ref, wo_ref, ng_ref, shf_ref, scf_ref, gf_ref,
                      wgu_ref, wd_ref, y_ref):
    d_a = a_ref.shape[1]
    m = _dot(a_ref[...], wo_ref[0:d_a, :]) + _dot(o_ref[...].astype(BF16), wo_ref[d_a:, :])
    x1 = x_ref[...] + g_ref[...] * m
    y_ref[...] = _ffn(x1, ng_ref[...], shf_ref[...], scf_ref[...], gf_ref[...], wgu_ref, wd_ref)


def _post_even(x, a, o, g, w_out, ng, shf, scf, gf, wgu, wd, tm):
    nb, t, d = x.shape
    return pl.pallas_call(
        _post_even_kernel,
        grid=(nb, t // tm),
        in_specs=[
            _rows_spec(tm, d), _rows_spec(tm, a.shape[2]), _rows_spec(tm, o.shape[2]),
            _mod_spec(g, tm, d), _const_spec(w_out.shape), _const_spec(ng.shape),
            _mod_spec(shf, tm, d), _mod_spec(scf, tm, d), _mod_spec(gf, tm, d),
            _layer_spec(wgu), _layer_spec(wd),
        ],
        out_specs=_rows_spec(tm, d),
        out_shape=jax.ShapeDtypeStruct((nb, t, d), F32),
        compiler_params=_cparams(2),
        name="post_even",
    )(x, a, o, g.mods, w_out, ng, shf.mods, scf.mods, gf.mods, wgu.stack, wd.stack)


def _odd_kernel(*refs, tm, short_rows, emit_v):
    (x_ref, sh_ref, sc_ref, ng0_ref, wi_ref, bi_ref, lg_ref, lb_ref, ws_ref, bimg_ref, wo_ref,
     g_ref, ng1_ref, shf_ref, scf_ref, gf_ref, wgu_ref, wd_ref) = refs[:18]
    if emit_v:
        y_ref, v_ref, mix_ref = refs[18:]
    else:
        y_ref, mix_ref = refs[18:]
    d_c = wo_ref.shape[0]
    gw = d_c // G_C

    x = x_ref[...]
    h = _rms_mod(x, ng0_ref[...], sh_ref[...], sc_ref[...]).astype(BF16)
    v = _gelu_tanh(_dot(h, wi_ref[:, d_c:]) + bi_ref[:, d_c:])
    mu = jnp.mean(v, axis=-1, keepdims=True)
    vc = v - mu
    var = jnp.mean(vc * vc, axis=-1, keepdims=True)
    vn = vc * lax.rsqrt(var + EPS) * lg_ref[...] + lb_ref[...]
    if emit_v:
        v_ref[...] = vn
    vb = vn.astype(BF16)
    u = _gelu_tanh(_dot(h, wi_ref[:, 0:d_c]) + bi_ref[:, 0:d_c])

    r = lax.broadcasted_iota(jnp.int32, (CHUNK, CHUNK), 0)
    c = lax.broadcasted_iota(jnp.int32, (CHUNK, CHUNK), 1)
    if short_rows is None:
        for gi in range(G_C):
            wsg = jnp.where(c <= r, ws_ref[gi], 0.0).astype(BF16)
            for n in range(tm // CHUNK):
                blk = _dot(wsg, vb[n * CHUNK:(n + 1) * CHUNK, gi * gw:(gi + 1) * gw])
                mix_ref[n * CHUNK:(n + 1) * CHUNK, gi * gw:(gi + 1) * gw] = (
                    blk + bimg_ref[:, gi * gw:(gi + 1) * gw])
    else:
        rr = lax.broadcasted_iota(jnp.int32, (tm, CHUNK), 0)
        cc = lax.broadcasted_iota(jnp.int32, (tm, CHUNK), 1)
        sel = jnp.where(jnp.bitwise_and(rr, short_rows - 1) == cc, 1.0, 0.0).astype(BF16)
        r2 = lax.broadcasted_iota(jnp.int32, (tm, tm), 0)
        c2 = lax.broadcasted_iota(jnp.int32, (tm, tm), 1)
        same_seq = _shr(r2, short_rows) == _shr(c2, short_rows)
        causal = jnp.bitwise_and(c2, short_rows - 1) <= jnp.bitwise_and(r2, short_rows - 1)
        keep = jnp.where(same_seq, jnp.where(causal, 1.0, 0.0), 0.0)
        for gi in range(G_C):
            spread = _dot_nt(_dot(sel, ws_ref[gi].astype(BF16)).astype(BF16), sel)
            mg = (spread * keep).astype(BF16)
            blk = _dot(mg, vb[:, gi * gw:(gi + 1) * gw])
            mix_ref[:, gi * gw:(gi + 1) * gw] = blk + bimg_ref[:, gi * gw:(gi + 1) * gw]

    y = _dot((u * mix_ref[...]).astype(BF16), wo_ref[...])
    x1 = x + g_ref[...] * y
    y_ref[...] = _ffn(x1, ng1_ref[...], shf_ref[...], scf_ref[...], gf_ref[...], wgu_ref, wd_ref)


def _odd(x, sh, sc, ng0, w_in, b_in, ln_g, ln_b, w_s, bimg, w_out, g, ng1, shf, scf, gf, wgu, wd,
         tm, short_rows, emit_v):
    nb, t, d = x.shape
    d_c = w_out.shape[0]
    out_specs = [_rows_spec(tm, d)]
    out_shape = [jax.ShapeDtypeStruct((nb, t, d), F32)]
    if emit_v:
        out_specs.append(_rows_spec(tm, d_c))
        out_shape.append(jax.ShapeDtypeStruct((nb, t, d_c), F32))
    return pl.pallas_call(
        functools.partial(_odd_kernel, tm=tm, short_rows=short_rows, emit_v=emit_v),
        grid=(nb, t // tm),
        in_specs=[
            _rows_spec(tm, d), _mod_spec(sh, tm, d), _mod_spec(sc, tm, d), _const_spec(ng0.shape),
            _const_spec(w_in.shape), _const_spec(b_in.shape), _const_spec(ln_g.shape),
            _const_spec(ln_b.shape), _const_spec(w_s.shape), _const_spec(bimg.shape),
            _const_spec(w_out.shape), _mod_spec(g, tm, d), _const_spec(ng1.shape),
            _mod_spec(shf, tm, d), _mod_spec(scf, tm, d), _mod_spec(gf, tm, d),
            _layer_spec(wgu), _layer_spec(wd),
        ],
        out_specs=out_specs, out_shape=out_shape,
        scratch_shapes=[pltpu.VMEM((tm, d_c), F32)],
        compiler_params=_cparams(2),
        name="odd",
    )(x, sh.mods, sc.mods, ng0, w_in, b_in, ln_g, ln_b, w_s, bimg, w_out, g.mods, ng1, shf.mods,
      scf.mods, gf.mods, wgu.stack, wd.stack)


def kernel(x_prompt, x_sample, c_prompt, c_sample, cache_k, cache_v, state_conv, page_table, w_ada, b_ada, norm_g, w_in_even, conv_w, q_norm_g, k_norm_g, lam_q1, lam_k1, lam_q2, lam_k2, subln_g, w_out_even, w_in_odd, b_in_odd, ln_g, ln_b, w_s, b_s, w_out_odd, w_gate_up, w_down):
    bp, tp, d = x_prompt.shape
    bs, ts, _ = x_sample.shape
    depth = w_ada.shape[0]
    d_a = conv_w.shape[2]
    qk_w = H_B * 2 * DK
    rows_s = bs * ts

    mods = _adaln(jnp.concatenate([c_prompt, c_sample], axis=0), w_ada, b_ada)

    grp = jnp.arange(MXU_TILE) // DK
    bd = jnp.where(grp[:, None] == grp[None, :], 1.0 / DK, 0.0).astype(BF16)

    wgu_all = w_gate_up.astype(BF16)
    wd_all = w_down.astype(BF16)
    xp = x_prompt
    xs = x_sample.reshape(1, rows_s, d)
    k_p, v_p, conv_p, k_s, v_s, conv_s, chunk_v = [], [], [], [], [], [], []

    for i in range(depth):
        mods_p = mods[i, :bp].reshape(bp, 1, -1)
        mods_s = jnp.repeat(mods[i, bp:], ts, axis=0).reshape(1, rows_s, -1)
        mp = [_ModChunk(mods_p, c) for c in range(6)]
        ms = [_ModChunk(mods_s, c) for c in range(6)]
        ng0 = norm_g[i, 0].reshape(1, d)
        ng1 = norm_g[i, 1].reshape(1, d)
        wgu = _LayerOf(wgu_all, i)
        wd = _LayerOf(wd_all, i)
        if i % 2 == 0:
            e = i // 2
            lam_init = 0.8 - 0.6 * math.exp(-0.3 * i)
            w_in = w_in_even[e].astype(BF16)
            w_out = w_out_even[e].astype(BF16)
            qg = jnp.tile(q_norm_g[e], qk_w // DK).reshape(1, qk_w)
            kg = jnp.tile(k_norm_g[e], qk_w // DK).reshape(1, qk_w)
            lamv = jnp.stack([lam_q1[e], lam_k1[e], lam_q2[e], lam_k2[e]])
            sg = subln_g[e].reshape(1, DV)

            a, q, k, v, kb, vt, tail = _inproj(xp, mp[0], mp[1], ng0, w_in, conv_w[e], qg, kg, bd,
                                               None, INPROJ_TILE)
            o = _attn(q, kb, vt, lamv, sg, lam_init, ATTN_TILE, ATTN_HEADS)
            xp = _post_even(xp, a, o, mp[2], w_out, ng1, mp[3], mp[4], mp[5], wgu, wd, ROW_TILE)
            k_p.append(k.reshape(bp, tp, H_B, 2 * DK))
            v_p.append(v.reshape(bp, tp, H_B, DV))
            conv_p.append(tail[:, 8 - (CONV_W - 1):, :])

            st = state_conv[e]
            zeros = jnp.zeros((bs, ts - 1, d_a), F32)
            fill1 = jnp.concatenate([st[:, 1:2], zeros], axis=1).reshape(1, rows_s, d_a)
            fill2 = jnp.concatenate([st, zeros[:, 1:]], axis=1).reshape(1, rows_s, d_a)
            a, q, k, v, _, _, xc = _inproj(xs, ms[0], ms[1], ng0, w_in, conv_w[e], qg, kg, bd,
                                           (fill1, fill2), rows_s)
            o = _sattn(page_table, q.astype(F32).reshape(bs, ts, qk_w), k.reshape(bs, ts, qk_w),
                       v.reshape(bs, ts, -1), lamv, sg, cache_k, cache_v, e, lam_init,
                       PAGES_PER_STEP)
            xs = _post_even(xs, a, o.reshape(1, rows_s, -1), ms[2], w_out, ng1, ms[3], ms[4], ms[5],
                            wgu, wd, rows_s)
            k_s.append(k.reshape(bs, ts, H_B, 2 * DK))
            v_s.append(v.reshape(bs, ts, H_B, DV))
            conv_s.append(xc.reshape(bs, ts, d_a)[:, ts - (CONV_W - 1):, :])
        else:
            oi = i // 2
            d_c = w_out_odd.shape[1]
            w_in = w_in_odd[oi].astype(BF16)
            w_out = w_out_odd[oi].astype(BF16)
            b_in = b_in_odd[oi].reshape(1, -1)
            lg = ln_g[oi].reshape(1, d_c)
            lb = ln_b[oi].reshape(1, d_c)
            bimg = jnp.repeat(b_s[oi].T, d_c // G_C, axis=1)
            xp = _odd(xp, mp[0], mp[1], ng0, w_in, b_in, lg, lb, w_s[oi], bimg, w_out, mp[2], ng1,
                      mp[3], mp[4], mp[5], wgu, wd, ROW_TILE, None, False)[0]
            bimg_s = jnp.tile(bimg[:ts], (bs, 1))
            xs, vrows = _odd(xs, ms[0], ms[1], ng0, w_in, b_in, lg, lb, w_s[oi], bimg_s, w_out, ms[2],
                             ng1, ms[3], ms[4], ms[5], wgu, wd, rows_s, ts, True)
            chunk_v.append(vrows.reshape(bs, ts, d_c))

    return (xp, xs.reshape(bs, ts, d), jnp.stack(k_p), jnp.stack(v_p), jnp.stack(conv_p),
            jnp.stack(k_s), jnp.stack(v_s), jnp.stack(conv_s), jnp.stack(chunk_v))
```

```python
import functools
import math
from typing import NamedTuple

import jax
import jax.numpy as jnp
from jax import lax
from jax.experimental import pallas as pl
from jax.experimental.pallas import tpu as pltpu

F32 = jnp.float32
BF16 = jnp.bfloat16
EPS = 1e-6

A_GROUPS = 8
CONV_W = 3
H_B = 4
DK = 64
DV = 2 * DK
CHUNK = 128
G_C = 8
Q_SCALE = DK ** -0.5 * math.log2(math.e)

VMEM_LIMIT_BYTES = 56 * 1024 * 1024
MXU_TILE = 256
BF16_ROWS = 16

ROW_TILE = 512
INPROJ_TILE = 1024
INPROJ_SUBTILES = 4
ATTN_TILE = 512
ATTN_COLS = 512
ATTN_HEADS = 4
PAGES_PER_STEP = 32


def _cparams(n_axes):
    return pltpu.CompilerParams(
        dimension_semantics=("arbitrary",) * n_axes,
        vmem_limit_bytes=VMEM_LIMIT_BYTES)


def _const_spec(shape):
    zeros = (0,) * len(shape)
    return pl.BlockSpec(shape, lambda *_: zeros, pipeline_mode=pl.Buffered(1))


class _LayerOf(NamedTuple):
    stack: jax.Array
    layer: int


def _layer_spec(w):
    shape = w.stack.shape
    index = (w.layer,) + (0,) * (len(shape) - 1)
    return pl.BlockSpec((None,) + shape[1:], lambda *_: index, pipeline_mode=pl.Buffered(1))


class _ModChunk(NamedTuple):
    mods: jax.Array
    chunk: int


def _mod_spec(m, tm, d):
    chunk = m.chunk
    if m.mods.shape[1] == 1:
        return pl.BlockSpec((None, 1, d), lambda b, i: (b, 0, chunk))
    return pl.BlockSpec((None, tm, d), lambda b, i: (b, i, chunk))


def _rows_spec(tm, width):
    return pl.BlockSpec((None, tm, width), lambda b, i: (b, i, 0))


def _dot(a, b):
    return jnp.dot(a, b, preferred_element_type=F32)


def _dot_nt(a, b):
    return lax.dot_general(a, b, (((1,), (1,)), ((), ())), preferred_element_type=F32)


def _mod_rows(ref, rows):
    return ref[...] if ref.shape[0] == 1 else ref[rows, :]


def _rms_mod(x, g, shift, scale):
    ms = jnp.mean(x * x, axis=-1, keepdims=True)
    y = x * lax.rsqrt(ms + EPS) * g
    return y * (1.0 + scale) + shift


def _shr(x, pow2):
    shift = pow2.bit_length() - 1
    assert 1 << shift == pow2
    return jnp.right_shift(x, shift)


def _silu(x):
    return x / (1.0 + jnp.exp(-x))


def _gelu_tanh(x):
    c = math.sqrt(2.0 / math.pi)
    half = 0.5 * x
    return half + half * jnp.tanh(x * (c + (c * 0.044715) * (x * x)))


def _ffn(x1, ng, shf, scf, gf, wgu_ref, wd_ref, h=None):
    d_ff = wd_ref.shape[0]
    n_tiles, rem = divmod(d_ff, MXU_TILE)
    assert rem == 0
    split = (n_tiles + 1) // 2 * MXU_TILE
    if h is None:
        h = _rms_mod(x1, ng, shf, scf).astype(BF16)
    acc = None
    for lo, hi in ((0, split), (split, d_ff)):
        g = _dot(h, wgu_ref[:, lo:hi])
        u = _dot(h, wgu_ref[:, d_ff + lo:d_ff + hi])
        a = (_silu(g) * u).astype(BF16)
        d = _dot(a, wd_ref[lo:hi, :])
        acc = d if acc is None else acc + d
    return x1 + gf * acc


def _adaln_kernel(c_ref, w_ref, b_ref, o_ref):
    s = _silu(c_ref[...])
    o_ref[...] = jnp.dot(s, w_ref[...], precision=lax.Precision.HIGHEST,
                         preferred_element_type=F32) + b_ref[...]


def _adaln(c_all, w_ada, b_ada):
    depth, d, d6 = w_ada.shape
    r = c_all.shape[0]
    tn = d6 // 4
    return pl.pallas_call(
        _adaln_kernel,
        grid=(depth, d6 // tn),
        in_specs=[
            pl.BlockSpec((r, d), lambda l, j: (0, 0)),
            pl.BlockSpec((None, d, tn), lambda l, j: (l, 0, j)),
            pl.BlockSpec((None, 1, tn), lambda l, j: (l, 0, j)),
        ],
        out_specs=pl.BlockSpec((None, r, tn), lambda l, j: (l, 0, j)),
        out_shape=jax.ShapeDtypeStruct((depth, r, d6), F32),
        compiler_params=_cparams(2),
        name="adaln",
    )(c_all, w_ada, b_ada.reshape(depth, 1, d6))


def _group_mean_sq(z, bd_ref):
    sq = (z * z).astype(BF16)
    w = bd_ref.shape[0]
    return jnp.concatenate([_dot(sq[:, c:c + w], bd_ref[...]) for c in range(0, z.shape[1], w)],
                           axis=1)


def _inproj_kernel(*refs, tm, n_sub, d_a, qk_w, has_fill, period):
    if has_fill:
        (x_ref, sh_ref, sc_ref, ng_ref, w_ref, cw_ref, qg_ref, kg_ref, bd_ref, f1_ref, f2_ref,
         a_ref, q_ref, k_ref, v_ref, kb_ref, vt_ref, tail_ref) = refs
    else:
        (x_ref, sh_ref, sc_ref, ng_ref, w_ref, cw_ref, qg_ref, kg_ref, bd_ref,
         a_ref, q_ref, k_ref, v_ref, kb_ref, vt_ref, tail_ref, carry_ref) = refs

    if not has_fill:
        @pl.when(pl.program_id(1) == 0)
        def _():
            carry_ref[...] = jnp.zeros_like(carry_ref)

    sub = tm // n_sub
    for r0 in range(0, tm, sub):
        rows = slice(r0, r0 + sub)
        h = _rms_mod(x_ref[rows, :], ng_ref[...], _mod_rows(sh_ref, rows),
                     _mod_rows(sc_ref, rows)).astype(BF16)

        a_b = _dot(h, w_ref[:, 0:d_a])
        xc = _dot(h, w_ref[:, d_a:2 * d_a]) * _dot(h, w_ref[:, 2 * d_a:3 * d_a])
        row = lax.broadcasted_iota(jnp.int32, (sub, d_a), 0)
        if has_fill:
            rp = jnp.bitwise_and(row, period - 1)
            fill1 = f1_ref[rows, :]
            fill2 = f2_ref[rows, :]
        else:
            rp = row
            last1 = carry_ref[pl.ds(7, 1), :]
            last2 = carry_ref[pl.ds(6, 1), :]
            fill1 = jnp.broadcast_to(last1, (sub, d_a))
            fill2 = jnp.where(row == 0, last2, last1)
        prev1 = jnp.where(rp < 1, fill1, pltpu.roll(xc, 1, 0))
        prev2 = jnp.where(rp < 2, fill2, pltpu.roll(xc, 2, 0))
        y = cw_ref[0:1, :] * prev2 + cw_ref[1:2, :] * prev1 + cw_ref[2:3, :] * xc
        a_ref[rows, :] = (a_b * y).astype(BF16)
        if has_fill:
            tail_ref[rows, :] = xc
        else:
            carry_ref[...] = xc[sub - 8:sub, :]
            if r0 + sub == tm:
                tail_ref[...] = xc[sub - 8:sub, :]

        q0 = 3 * d_a
        zq = _dot(h, w_ref[:, q0:q0 + qk_w])
        msq = _group_mean_sq(zq, bd_ref)
        qn = zq * lax.rsqrt(msq + EPS) * qg_ref[...]
        q_ref[rows, :] = (qn * Q_SCALE).astype(BF16)
        zk = _dot(h, w_ref[:, q0 + qk_w:q0 + 2 * qk_w])
        msk = _group_mean_sq(zk, bd_ref)
        kn = zk * lax.rsqrt(msk + EPS) * kg_ref[...]
        kb_ref[rows, :] = kn.astype(BF16)
        zv = _dot(h, w_ref[:, q0 + 2 * qk_w:])
        for out_ref, val in ((k_ref, kn), (v_ref, zv)):
            if has_fill:
                out_ref[rows, :] = val
            else:
                heads = val.shape[1] // DV
                for hh in range(heads):
                    out_ref[pl.ds(r0 * heads + hh, sub, stride=heads), :] = val[:, hh * DV:(hh + 1) * DV]
        vt_ref[:, rows] = zv.T.astype(BF16)


def _inproj(x, sh, sc, ng, w_in, cw, qg, kg, bd, fills, tm):
    nb, t, d = x.shape
    d_a = cw.shape[1]
    qk_w = qg.shape[1]
    d_b = w_in.shape[1] - 3 * d_a - 2 * qk_w
    has_fill = fills is not None
    in_specs = [
        _rows_spec(tm, d), _mod_spec(sh, tm, d), _mod_spec(sc, tm, d),
        _const_spec(ng.shape), _const_spec(w_in.shape), _const_spec(cw.shape),
        _const_spec(qg.shape), _const_spec(kg.shape), _const_spec(bd.shape),
    ]
    args = [x, sh.mods, sc.mods, ng, w_in, cw, qg, kg, bd]
    scratch = []
    if has_fill:
        in_specs += [_rows_spec(tm, d_a), _rows_spec(tm, d_a)]
        args += list(fills)
        tail_spec = _rows_spec(tm, d_a)
        tail_shape = (nb, t, d_a)
    else:
        scratch = [pltpu.VMEM((8, d_a), F32)]
        tail_spec = pl.BlockSpec((None, 8, d_a), lambda b, i: (b, 0, 0))
        tail_shape = (nb, 8, d_a)
    if has_fill:
        kv_specs = [_rows_spec(tm, qk_w), _rows_spec(tm, d_b)]
        kv_shapes = [(nb, t, qk_w), (nb, t, d_b)]
    else:
        kv_specs = [_rows_spec(tm * (qk_w // DV), DV), _rows_spec(tm * (d_b // DV), DV)]
        kv_shapes = [(nb, t * (qk_w // DV), DV), (nb, t * (d_b // DV), DV)]
    out_specs = [_rows_spec(tm, d_a), _rows_spec(tm, qk_w), *kv_specs,
                 _rows_spec(tm, qk_w), pl.BlockSpec((None, d_b, tm), lambda b, i: (b, 0, i)), tail_spec]
    out_shape = [
        jax.ShapeDtypeStruct((nb, t, d_a), BF16), jax.ShapeDtypeStruct((nb, t, qk_w), BF16),
        jax.ShapeDtypeStruct(kv_shapes[0], F32), jax.ShapeDtypeStruct(kv_shapes[1], F32),
        jax.ShapeDtypeStruct((nb, t, qk_w), BF16), jax.ShapeDtypeStruct((nb, d_b, t), BF16),
        jax.ShapeDtypeStruct(tail_shape, F32),
    ]
    return pl.pallas_call(
        functools.partial(_inproj_kernel, tm=tm, n_sub=1 if has_fill else INPROJ_SUBTILES, d_a=d_a,
                          qk_w=qk_w, has_fill=has_fill, period=8),
        grid=(nb, t // tm),
        in_specs=in_specs, out_specs=out_specs, out_shape=out_shape,
        scratch_shapes=scratch,
        compiler_params=_cparams(2),
        name="inproj",
    )(*args)


def _diff_lambda(lam_ref, lam_init):
    lv = lam_ref[...]
    e1 = jnp.exp(jnp.sum(lv[0:1, :] * lv[1:2, :], axis=-1, keepdims=True))
    e2 = jnp.exp(jnp.sum(lv[2:3, :] * lv[3:4, :], axis=-1, keepdims=True))
    return e1 - e2 + lam_init


def _subln(o, g, lam_init):
    ms = jnp.mean(o * o, axis=-1, keepdims=True)
    return o * lax.rsqrt(ms + EPS) * g * (1.0 - lam_init)


def _softmax_step(s, v_dot, m_ref, l_ref, acc_ref, rows=slice(None)):
    m_prev = m_ref[rows, :]
    m_new = jnp.maximum(m_prev, jnp.max(s, axis=-1, keepdims=True))
    alpha = jnp.exp2(m_prev - m_new)
    p = jnp.exp2(s - m_new)
    l_ref[rows, :] = alpha * l_ref[rows, :] + jnp.sum(p, axis=-1, keepdims=True)
    acc_ref[rows, :] = alpha * acc_ref[rows, :] + v_dot(p.astype(BF16))
    m_ref[rows, :] = m_new


def _attn_kernel(q_ref, k_ref, vt_ref, lam_ref, sg_ref, o_ref, qs_ref, sa_ref, sb_ref, m_ref, acc_ref,
                 *, tq, hps, lam_init):
    i = pl.program_id(2)
    lane = lax.broadcasted_iota(jnp.int32, (tq, DV), 1)
    for hh in range(hps):
        q = q_ref[:, hh * DV:(hh + 1) * DV]
        zero = jnp.zeros_like(q)
        qs_ref[hh, 0:tq, :] = jnp.where(lane < DK, q, zero)
        qs_ref[hh, tq:2 * tq, :] = jnp.where(lane >= DK, q, zero)
    m_ref[...] = jnp.full_like(m_ref, -jnp.inf)
    acc_ref[...] = jnp.zeros_like(acc_ref)

    chains = [(hh, c0, hh * 2 * tq + c0) for hh in range(hps) for c0 in range(0, 2 * tq, ATTN_COLS)]
    ones = jnp.ones((acc_ref.shape[0] - DV, tq), BF16)

    def scores(j, s_ref):
        keys = pl.ds(pl.multiple_of(j * tq, tq), tq)
        for hh, c0, g0 in chains:
            k = k_ref[keys, hh * DV:(hh + 1) * DV]
            s_ref[:, g0:g0 + ATTN_COLS] = _dot_nt(k, qs_ref[hh, c0:c0 + ATTN_COLS, :])

    def step(j, s_ref, diagonal):
        keys = pl.ds(pl.multiple_of(j * tq, tq), tq)
        vts = [jnp.concatenate([vt_ref[hh * DV:(hh + 1) * DV, keys], ones], axis=0)
               for hh in range(hps)]
        for hh, c0, g0 in chains:
            cols = slice(g0, g0 + ATTN_COLS)
            st = s_ref[:, cols]
            if diagonal:
                key = lax.broadcasted_iota(jnp.int32, st.shape, 0)
                qrow = lax.broadcasted_iota(jnp.int32, st.shape, 1) + (c0 % tq)
                st = jnp.where(key <= qrow, st, -jnp.inf)
            m_prev = m_ref[:, cols]
            m_new = jnp.maximum(m_prev, jnp.max(st, axis=0, keepdims=True))
            alpha = jnp.exp2(m_prev - m_new)
            pt = jnp.exp2(st - m_new)
            acc_ref[:, cols] = alpha * acc_ref[:, cols] + _dot(vts[hh], pt.astype(BF16))
            m_ref[:, cols] = m_new

    scores(0, sa_ref)

    def pair(jj, carry):
        j0 = 2 * jj
        scores(j0 + 1, sb_ref)
        step(j0, sa_ref, False)
        scores(j0 + 2, sa_ref)
        step(j0 + 1, sb_ref, False)
        return carry

    lax.fori_loop(0, i // 2, pair, 0)

    @pl.when(i % 2 == 0)
    def _():
        step(i, sa_ref, True)

    @pl.when(i % 2 == 1)
    def _():
        scores(i, sb_ref)
        step(i - 1, sa_ref, False)
        step(i, sb_ref, True)

    lam = _diff_lambda(lam_ref, lam_init)
    for hh in range(hps):
        h0 = hh * 2 * tq
        nt = acc_ref[0:DV, h0:h0 + 2 * tq] / acc_ref[DV:DV + 1, h0:h0 + 2 * tq]
        ot = nt[:, 0:tq] - lam * nt[:, tq:2 * tq]
        ms = jnp.mean(ot * ot, axis=0, keepdims=True)
        ot = ot * lax.rsqrt(ms + EPS)
        o_ref[:, hh * DV:(hh + 1) * DV] = (ot.T * sg_ref[...] * (1.0 - lam_init)).astype(o_ref.dtype)


def _attn(q, kb, vt, lamv, sg, lam_init, tq, hps):
    b, t, w = q.shape
    hw = hps * DV
    return pl.pallas_call(
        functools.partial(_attn_kernel, tq=tq, hps=hps, lam_init=lam_init),
        grid=(b, w // hw, t // tq),
        in_specs=[
            pl.BlockSpec((None, tq, hw), lambda b_, h, i: (b_, i, h)),
            pl.BlockSpec((None, t, hw), lambda b_, h, i: (b_, 0, h)),
            pl.BlockSpec((None, hw, t), lambda b_, h, i: (b_, h, 0)),
            _const_spec(lamv.shape), _const_spec(sg.shape),
        ],
        out_specs=pl.BlockSpec((None, tq, hw), lambda b_, h, i: (b_, i, h)),
        out_shape=jax.ShapeDtypeStruct((b, t, w), BF16),
        scratch_shapes=[
            pltpu.VMEM((hps, 2 * tq, DV), BF16),
            pltpu.VMEM((tq, hps * 2 * tq), F32), pltpu.VMEM((tq, hps * 2 * tq), F32),
            pltpu.VMEM((1, hps * 2 * tq), F32),
            pltpu.VMEM((DV + BF16_ROWS, hps * 2 * tq), F32),
        ],
        compiler_params=_cparams(3),
        name="attn",
    )(q, kb, vt, lamv, sg)


def _sattn_kernel(pt_ref, q_ref, kn_ref, vn_ref, lam_ref, sg_ref, *rest, pp, ts, lam_init):
    k_refs = rest[0:pp]
    v_refs = rest[pp:2 * pp]
    o_ref, qs_ref, m_ref, l_ref, acc_ref = rest[2 * pp:]
    j = pl.program_id(1)
    rows = qs_ref.shape[0]
    hr = 2 * ts
    heads = rows // hr
    page = k_refs[0].shape[0] // heads

    @pl.when(j == 0)
    def _():
        q = q_ref[...]
        lane = lax.broadcasted_iota(jnp.int32, (ts, DV), 1)
        parts = []
        for h in range(heads):
            qh = q[:, h * DV:(h + 1) * DV]
            parts += [jnp.where(lane < DK, qh, 0.0), jnp.where(lane >= DK, qh, 0.0)]
        qs = jnp.concatenate(parts, axis=0).astype(BF16)
        qs_ref[...] = qs
        m_ref[...] = jnp.full_like(m_ref, -jnp.inf)
        l_ref[...] = jnp.zeros_like(l_ref)
        acc_ref[...] = jnp.zeros_like(acc_ref)
        pad = jnp.zeros((DV - ts, DV), F32)
        kns = [jnp.concatenate([kn_ref[:, h * DV:(h + 1) * DV], pad], axis=0).astype(BF16)
               for h in range(heads)]
        vns = [jnp.concatenate([vn_ref[:, h * DV:(h + 1) * DV], pad], axis=0).astype(BF16)
               for h in range(heads)]
        s = jnp.concatenate([_dot_nt(qs[h * hr:(h + 1) * hr, :], kns[h]) for h in range(heads)],
                            axis=0)
        rr = lax.broadcasted_iota(jnp.int32, s.shape, 0)
        cc = lax.broadcasted_iota(jnp.int32, s.shape, 1)
        s = jnp.where(cc <= jnp.bitwise_and(rr, ts - 1), s, -jnp.inf)

        def new_v_dot(pb):
            return jnp.concatenate([_dot(pb[h * hr:(h + 1) * hr, :], vns[h]) for h in range(heads)],
                                   axis=0)

        _softmax_step(s, new_v_dot, m_ref, l_ref, acc_ref)

    def head_rows(refs, h):
        return jnp.concatenate([r[pl.ds(h, page, stride=heads), :] for r in refs],
                               axis=0).astype(BF16)

    hrows = [slice(h * hr, (h + 1) * hr) for h in range(heads)]
    ss = [_dot_nt(qs_ref[hrows[h], :], head_rows(k_refs, h)) for h in range(heads)]
    for h in range(heads):
        _softmax_step(ss[h], lambda pb: _dot(pb, head_rows(v_refs, h)), m_ref, l_ref, acc_ref,
                      hrows[h])

    @pl.when(j == pl.num_programs(1) - 1)
    def _():
        lam = _diff_lambda(lam_ref, lam_init)
        n = acc_ref[...] / l_ref[...]
        outs = []
        for h in range(heads):
            o = n[h * hr:h * hr + ts, :] - lam * n[h * hr + ts:(h + 1) * hr, :]
            outs.append(_subln(o, sg_ref[...], lam_init))
        o_ref[...] = jnp.concatenate(outs, axis=1)


def _sattn(page_table, q, kn, vn, lamv, sg, cache_k, cache_v, layer, lam_init, pp):
    bs, ts, w = q.shape
    n_pages = page_table.shape[1]
    prow = cache_k.shape[2] * cache_k.shape[3]
    ck = cache_k.reshape(cache_k.shape[0], cache_k.shape[1], prow, DV)
    cv = cache_v.reshape(cache_v.shape[0], cache_v.shape[1], prow, DV)
    rows = (w // DK) * ts

    def page_spec(p):
        return pl.BlockSpec((None, None, prow, DV),
                            lambda b, j, pt: (layer, pt[b, j * pp + p], 0, 0))

    new_spec = pl.BlockSpec((None, ts, w), lambda b, j, pt: (b, 0, 0))
    grid_spec = pltpu.PrefetchScalarGridSpec(
        num_scalar_prefetch=1,
        grid=(bs, n_pages // pp),
        in_specs=[new_spec, new_spec, new_spec, _const_spec(lamv.shape), _const_spec(sg.shape)]
        + [page_spec(p) for p in range(pp)] + [page_spec(p) for p in range(pp)],
        out_specs=new_spec,
        scratch_shapes=[
            pltpu.VMEM((rows, DV), BF16),
            pltpu.VMEM((rows, 1), F32), pltpu.VMEM((rows, 1), F32),
            pltpu.VMEM((rows, DV), F32),
        ],
    )
    return pl.pallas_call(
        functools.partial(_sattn_kernel, pp=pp, ts=ts, lam_init=lam_init),
        grid_spec=grid_spec,
        out_shape=jax.ShapeDtypeStruct((bs, ts, w), F32),
        compiler_params=_cparams(2),
        name="sattn",
    )(page_table, q, kn, vn, lamv, sg, *([ck] * pp), *([cv] * pp))


def _post_even_kernel(x_ref, a_ref, o_ref, g_ref, wo_ref, ng_ref, shf_ref, scf_ref, gf_ref,
                      wgu_ref, wd_ref, y_ref):
    d_a = a_ref.shape[1]
    half = x_ref.shape[0] // 2
    x1s, hs = [], []
    for r0 in (0, half):
        rows = slice(r0, r0 + half)
        m = (_dot(a_ref[rows, :], wo_ref[0:d_a, :])
             + _dot(o_ref[rows, :].astype(BF16), wo_ref[d_a:, :]))
        x1 = x_ref[rows, :] + _mod_rows(g_ref, rows) * m
        x1s.append(x1)
        hs.append(_rms_mod(x1, ng_ref[...], _mod_rows(shf_ref, rows), _mod_rows(scf_ref, rows)).astype(BF16))
    y_ref[...] = _ffn(jnp.concatenate(x1s, axis=0), ng_ref[...], shf_ref[...], scf_ref[...], gf_ref[...],
                      wgu_ref, wd_ref, h=jnp.concatenate(hs, axis=0))


def _post_even(x, a, o, g, w_out, ng, shf, scf, gf, wgu, wd, tm):
    nb, t, d = x.shape
    return pl.pallas_call(
        _post_even_kernel,
        grid=(nb, t // tm),
        in_specs=[
            _rows_spec(tm, d), _rows_spec(tm, a.shape[2]), _rows_spec(tm, o.shape[2]),
            _mod_spec(g, tm, d), _const_spec(w_out.shape), _const_spec(ng.shape),
            _mod_spec(shf, tm, d), _mod_spec(scf, tm, d), _mod_spec(gf, tm, d),
            _layer_spec(wgu), _layer_spec(wd),
        ],
        out_specs=_rows_spec(tm, d),
        out_shape=jax.ShapeDtypeStruct((nb, t, d), F32),
        compiler_params=_cparams(2),
        name="post_even",
    )(x, a, o, g.mods, w_out, ng, shf.mods, scf.mods, gf.mods, wgu.stack, wd.stack)


def _odd_kernel(*refs, tm, short_rows, emit_v):
    (x_ref, sh_ref, sc_ref, ng0_ref, wi_ref, bi_ref, lg_ref, lb_ref, ws_ref, bimg_ref, wo_ref,
     g_ref, ng1_ref, shf_ref, scf_ref, gf_ref, wgu_ref, wd_ref) = refs[:18]
    if emit_v:
        y_ref, v_ref, mix_ref = refs[18:]
    else:
        y_ref, mix_ref = refs[18:]
    d_c = wo_ref.shape[0]
    gw = d_c // G_C

    x = x_ref[...]
    h = _rms_mod(x, ng0_ref[...], sh_ref[...], sc_ref[...]).astype(BF16)
    v = _gelu_tanh(_dot(h, wi_ref[:, d_c:]) + bi_ref[:, d_c:])
    mu = jnp.mean(v, axis=-1, keepdims=True)
    vc = v - mu
    var = jnp.mean(vc * vc, axis=-1, keepdims=True)
    vn = vc * lax.rsqrt(var + EPS) * lg_ref[...] + lb_ref[...]
    if emit_v:
        v_ref[...] = vn
    vb = vn.astype(BF16)
    u = _gelu_tanh(_dot(h, wi_ref[:, 0:d_c]) + bi_ref[:, 0:d_c])

    r = lax.broadcasted_iota(jnp.int32, (CHUNK, CHUNK), 0)
    c = lax.broadcasted_iota(jnp.int32, (CHUNK, CHUNK), 1)
    if short_rows is None:
        for gi in range(G_C):
            wsg = jnp.where(c <= r, ws_ref[gi], 0.0).astype(BF16)
            for n in range(tm // CHUNK):
                blk = _dot(wsg, vb[n * CHUNK:(n + 1) * CHUNK, gi * gw:(gi + 1) * gw])
                mix_ref[n * CHUNK:(n + 1) * CHUNK, gi * gw:(gi + 1) * gw] = (
                    blk + bimg_ref[:, gi * gw:(gi + 1) * gw])
    else:
        rr = lax.broadcasted_iota(jnp.int32, (tm, CHUNK), 0)
        cc = lax.broadcasted_iota(jnp.int32, (tm, CHUNK), 1)
        sel = jnp.where(jnp.bitwise_and(rr, short_rows - 1) == cc, 1.0, 0.0).astype(BF16)
        r2 = lax.broadcasted_iota(jnp.int32, (tm, tm), 0)
        c2 = lax.broadcasted_iota(jnp.int32, (tm, tm), 1)
        same_seq = _shr(r2, short_rows) == _shr(c2, short_rows)
        causal = jnp.bitwise_and(c2, short_rows - 1) <= jnp.bitwise_and(r2, short_rows - 1)
        keep = jnp.where(same_seq, jnp.where(causal, 1.0, 0.0), 0.0)
        for gi in range(G_C):
            spread = _dot_nt(_dot(sel, ws_ref[gi].astype(BF16)).astype(BF16), sel)
            mg = (spread * keep).astype(BF16)
            blk = _dot(mg, vb[:, gi * gw:(gi + 1) * gw])
            mix_ref[:, gi * gw:(gi + 1) * gw] = blk + bimg_ref[:, gi * gw:(gi + 1) * gw]

    y = _dot((u * mix_ref[...]).astype(BF16), wo_ref[...])
    x1 = x + g_ref[...] * y
    y_ref[...] = _ffn(x1, ng1_ref[...], shf_ref[...], scf_ref[...], gf_ref[...], wgu_ref, wd_ref)


def _odd(x, sh, sc, ng0, w_in, b_in, ln_g, ln_b, w_s, bimg, w_out, g, ng1, shf, scf, gf, wgu, wd,
         tm, short_rows, emit_v):
    nb, t, d = x.shape
    d_c = w_out.shape[0]
    out_specs = [_rows_spec(tm, d)]
    out_shape = [jax.ShapeDtypeStruct((nb, t, d), F32)]
    if emit_v:
        out_specs.append(_rows_spec(tm, d_c))
        out_shape.append(jax.ShapeDtypeStruct((nb, t, d_c), F32))
    return pl.pallas_call(
        functools.partial(_odd_kernel, tm=tm, short_rows=short_rows, emit_v=emit_v),
        grid=(nb, t // tm),
        in_specs=[
            _rows_spec(tm, d), _mod_spec(sh, tm, d), _mod_spec(sc, tm, d), _const_spec(ng0.shape),
            _const_spec(w_in.shape), _const_spec(b_in.shape), _const_spec(ln_g.shape),
            _const_spec(ln_b.shape), _const_spec(w_s.shape), _const_spec(bimg.shape),
            _const_spec(w_out.shape), _mod_spec(g, tm, d), _const_spec(ng1.shape),
            _mod_spec(shf, tm, d), _mod_spec(scf, tm, d), _mod_spec(gf, tm, d),
            _layer_spec(wgu), _layer_spec(wd),
        ],
        out_specs=out_specs, out_shape=out_shape,
        scratch_shapes=[pltpu.VMEM((tm, d_c), F32)],
        compiler_params=_cparams(2),
        name="odd",
    )(x, sh.mods, sc.mods, ng0, w_in, b_in, ln_g, ln_b, w_s, bimg, w_out, g.mods, ng1, shf.mods,
      scf.mods, gf.mods, wgu.stack, wd.stack)


def kernel(x_prompt, x_sample, c_prompt, c_sample, cache_k, cache_v, state_conv, page_table, w_ada, b_ada, norm_g, w_in_even, conv_w, q_norm_g, k_norm_g, lam_q1, lam_k1, lam_q2, lam_k2, subln_g, w_out_even, w_in_odd, b_in_odd, ln_g, ln_b, w_s, b_s, w_out_odd, w_gate_up, w_down):
    bp, tp, d = x_prompt.shape
    bs, ts, _ = x_sample.shape
    depth = w_ada.shape[0]
    d_a = conv_w.shape[2]
    qk_w = H_B * 2 * DK
    rows_s = bs * ts

    mods = _adaln(jnp.concatenate([c_prompt, c_sample], axis=0), w_ada, b_ada)

    grp = jnp.arange(MXU_TILE) // DK
    bd = jnp.where(grp[:, None] == grp[None, :], 1.0 / DK, 0.0).astype(BF16)

    wgu_all = w_gate_up.astype(BF16)
    wd_all = w_down.astype(BF16)
    xp = x_prompt
    xs = x_sample.reshape(1, rows_s, d)
    k_p, v_p, conv_p, k_s, v_s, conv_s, chunk_v = [], [], [], [], [], [], []

    for i in range(depth):
        mods_p = mods[i, :bp].reshape(bp, 1, -1)
        mods_s = jnp.repeat(mods[i, bp:], ts, axis=0).reshape(1, rows_s, -1)
        mp = [_ModChunk(mods_p, c) for c in range(6)]
        ms = [_ModChunk(mods_s, c) for c in range(6)]
        ng0 = norm_g[i, 0].reshape(1, d)
        ng1 = norm_g[i, 1].reshape(1, d)
        wgu = _LayerOf(wgu_all, i)
        wd = _LayerOf(wd_all, i)
        if i % 2 == 0:
            e = i // 2
            lam_init = 0.8 - 0.6 * math.exp(-0.3 * i)
            w_in = w_in_even[e].astype(BF16)
            w_out = w_out_even[e].astype(BF16)
            qg = jnp.tile(q_norm_g[e], qk_w // DK).reshape(1, qk_w)
            kg = jnp.tile(k_norm_g[e], qk_w // DK).reshape(1, qk_w)
            lamv = jnp.stack([lam_q1[e], lam_k1[e], lam_q2[e], lam_k2[e]])
            sg = subln_g[e].reshape(1, DV)

            a, q, k, v, kb, vt, tail = _inproj(xp, mp[0], mp[1], ng0, w_in, conv_w[e], qg, kg, bd,
                                               None, INPROJ_TILE)
            o = _attn(q, kb, vt, lamv, sg, lam_init, ATTN_TILE, ATTN_HEADS)
            xp = _post_even(xp, a, o, mp[2], w_out, ng1, mp[3], mp[4], mp[5], wgu, wd, ROW_TILE)
            k_p.append(k.reshape(bp, tp, H_B, 2 * DK))
            v_p.append(v.reshape(bp, tp, H_B, DV))
            conv_p.append(tail[:, 8 - (CONV_W - 1):, :])

            st = state_conv[e]
            zeros = jnp.zeros((bs, ts - 1, d_a), F32)
            fill1 = jnp.concatenate([st[:, 1:2], zeros], axis=1).reshape(1, rows_s, d_a)
            fill2 = jnp.concatenate([st, zeros[:, 1:]], axis=1).reshape(1, rows_s, d_a)
            a, q, k, v, _, _, xc = _inproj(xs, ms[0], ms[1], ng0, w_in, conv_w[e], qg, kg, bd,
                                           (fill1, fill2), rows_s)
            o = _sattn(page_table, q.astype(F32).reshape(bs, ts, qk_w), k.reshape(bs, ts, qk_w),
                       v.reshape(bs, ts, -1), lamv, sg, cache_k, cache_v, e, lam_init,
                       PAGES_PER_STEP)
            xs = _post_even(xs, a, o.reshape(1, rows_s, -1), ms[2], w_out, ng1, ms[3], ms[4], ms[5],
                            wgu, wd, rows_s)
            k_s.append(k.reshape(bs, ts, H_B, 2 * DK))
            v_s.append(v.reshape(bs, ts, H_B, DV))
            conv_s.append(xc.reshape(bs, ts, d_a)[:, ts - (CONV_W - 1):, :])
        else:
            oi = i // 2
            d_c = w_out_odd.shape[1]
            w_in = w_in_odd[oi].astype(BF16)
            w_out = w_out_odd[oi].astype(BF16)
            b_in = b_in_odd[oi].reshape(1, -1)
            lg = ln_g[oi].reshape(1, d_c)
            lb = ln_b[oi].reshape(1, d_c)
            bimg = jnp.repeat(b_s[oi].T, d_c // G_C, axis=1)
            xp = _odd(xp, mp[0], mp[1], ng0, w_in, b_in, lg, lb, w_s[oi], bimg, w_out, mp[2], ng1,
                      mp[3], mp[4], mp[5], wgu, wd, ROW_TILE, None, False)[0]
            bimg_s = jnp.tile(bimg[:ts], (bs, 1))
            xs, vrows = _odd(xs, ms[0], ms[1], ng0, w_in, b_in, lg, lb, w_s[oi], bimg_s, w_out, ms[2],
                             ng1, ms[3], ms[4], ms[5], wgu, wd, rows_s, ts, True)
            chunk_v.append(vrows.reshape(bs, ts, d_c))

    return (xp, xs.reshape(bs, ts, d), jnp.stack(k_p), jnp.stack(v_p), jnp.stack(conv_p),
            jnp.stack(k_s), jnp.stack(v_s), jnp.stack(conv_s), jnp.stack(chunk_v))
```
